```python
import jax, jax.numpy as jnp
from jax import lax
import numpy as np

D_MODEL = 1024
BATCH = 8
SEQ = 4096
DEPTH = 2

CHUNK = 64
QBLOCK = 128
EPS = 1e-6

D_MIX_A = D_MODEL
HGRN_EXPAND = 128
HGRN_HEADS = D_MIX_A // HGRN_EXPAND
HGRN_DK = HGRN_EXPAND
HGRN_DV = D_MIX_A // HGRN_HEADS
D_MIX_B = D_MODEL
SB_HEAD_DIM = 64
SB_HEADS = D_MIX_B // SB_HEAD_DIM
D_MIX = D_MIX_A + D_MIX_B

IN_WIDTHS = (HGRN_HEADS * HGRN_DK, HGRN_HEADS * HGRN_DK, D_MIX_A, D_MIX_A,
             D_MIX_B, D_MIX_B, D_MIX_B, D_MIX_B)
D_IN = sum(IN_WIDTHS)
IN_OFFSETS = tuple(int(v) for v in np.cumsum(IN_WIDTHS)[:-1])

kernel_name = "hybrid_hgrn2_stickbreaking_adaln_trunk"


def rms_norm(x, g):
    xf = x.astype(jnp.float32)
    y = xf * lax.rsqrt(jnp.mean(xf * xf, axis=-1, keepdims=True) + EPS)
    return (y * g.astype(jnp.float32)).astype(x.dtype)


def hgrn2_mixer(q, f_logit, i, lb):
    bsz, seq = q.shape[0], q.shape[1]
    n_chunks = seq // CHUNK
    q = jax.nn.silu(q.astype(jnp.float32))
    fl = f_logit.astype(jnp.float32)
    lb = lb.astype(jnp.float32)
    log_f = jnp.logaddexp(jnp.log(lb), jnp.log1p(-lb) + jax.nn.log_sigmoid(fl))
    k = (1.0 - lb) * jax.nn.sigmoid(-fl)

    def to_chunks(t, d):
        return t.reshape(bsz, n_chunks, CHUNK, HGRN_HEADS, d).transpose(1, 0, 3, 2, 4)

    qc = to_chunks(q, HGRN_DK)
    kc = to_chunks(k, HGRN_DK)
    ic = to_chunks(i.astype(jnp.float32), HGRN_DV)
    bc = jnp.cumsum(to_chunks(log_f, HGRN_DK), axis=3)
    causal = jnp.tril(jnp.ones((CHUNK, CHUNK), dtype=bool))[:, :, None]

    def step(state, xs):
        q_, k_, i_, b_ = xs
        diff = b_[:, :, :, None, :] - b_[:, :, None, :, :]
        decay = jnp.where(causal, jnp.exp(jnp.where(causal, diff, 0.0)), 0.0)
        scores = jnp.einsum('bhtd,bhsd,bhtsd->bhts', q_, k_, decay)
        b_last = b_[:, :, -1, :]
        o = (jnp.einsum('bhts,bhsv->bhtv', scores, i_)
             + jnp.einsum('bhtd,bhdv->bhtv', q_ * jnp.exp(b_), state))
        k_dec = k_ * jnp.exp(b_last[:, :, None, :] - b_)
        state = jnp.exp(b_last)[..., None] * state + jnp.einsum('bhsd,bhsv->bhdv', k_dec, i_)
        return state, o

    s0 = jnp.zeros((bsz, HGRN_HEADS, HGRN_DK, HGRN_DV), jnp.float32)
    _, o = lax.scan(step, s0, (qc, kc, ic, bc))
    return o.transpose(1, 0, 3, 2, 4).reshape(bsz, seq, HGRN_HEADS, HGRN_DV)


def stick_breaking_attention(q, k, v):
    seq = q.shape[1]
    scale = SB_HEAD_DIM ** -0.5
    qh = q.astype(jnp.float32).transpose(0, 2, 1, 3)
    kh = k.astype(jnp.float32).transpose(0, 2, 1, 3)
    vh = v.astype(jnp.float32).transpose(0, 2, 1, 3)
    outs = []
    for nb in range(seq // QBLOCK):
        t0, t1 = nb * QBLOCK, (nb + 1) * QBLOCK
        z = jnp.einsum('bhtd,bhsd->bhts', qh[:, :, t0:t1], kh[:, :, :t1]) * scale
        mask = jnp.arange(t1)[None, :] < (t0 + jnp.arange(QBLOCK))[:, None]
        log_keep = jnp.where(mask, jax.nn.log_sigmoid(-z), 0.0)
        rc = lax.cumsum(log_keep, axis=3, reverse=True)
        after = jnp.concatenate([rc[..., 1:], jnp.zeros_like(rc[..., :1])], axis=-1)
        w = jnp.where(mask, jnp.exp(jax.nn.log_sigmoid(z) + after), 0.0)
        outs.append(jnp.einsum('bhts,bhsd->bhtd', w, vh[:, :, :t1]))
    o = jnp.concatenate(outs, axis=2)
    return o.transpose(0, 2, 1, 3)


def setup_inputs(seed: int = 0) -> dict:
    key = jax.random.key(seed)
    ks = jax.random.split(key, 12)
    f32 = jnp.float32
    x = jax.random.normal(ks[0], (BATCH, SEQ, D_MODEL), f32)
    c = jax.random.normal(ks[1], (BATCH, D_MODEL), f32)
    w_ada = jax.random.normal(ks[2], (DEPTH, D_MODEL, 3 * D_MODEL), f32) * (0.5 * D_MODEL ** -0.5)
    b_ada = jax.random.normal(ks[3], (DEPTH, 3 * D_MODEL), f32) * 0.01
    norm_g = 1.0 + 0.01 * jax.random.normal(ks[4], (DEPTH, D_MODEL), f32)
    w_in = jax.random.normal(ks[5], (DEPTH, D_MODEL, D_IN), f32) * D_MODEL ** -0.5
    lb_logits = jax.random.normal(ks[6], (DEPTH, HGRN_HEADS * HGRN_DK), f32)
    hgrn_norm_g = 1.0 + 0.01 * jax.random.normal(ks[7], (DEPTH, HGRN_DV), f32)
    sb_norm_g = 1.0 + 0.01 * jax.random.normal(ks[8], (DEPTH, SB_HEAD_DIM), f32)
    w_out = jax.random.normal(ks[9], (DEPTH, D_MIX, D_MODEL), f32) * D_MIX ** -0.5
    final_g = 1.0 + 0.01 * jax.random.normal(ks[10], (D_MODEL,), f32)
    return {"x": x, "c": c, "w_ada": w_ada, "b_ada": b_ada, "norm_g": norm_g,
            "w_in": w_in, "lb_logits": lb_logits, "hgrn_norm_g": hgrn_norm_g,
            "sb_norm_g": sb_norm_g, "w_out": w_out, "final_g": final_g}


def reference(x, c, w_ada, b_ada, norm_g, w_in, lb_logits, hgrn_norm_g, sb_norm_g, w_out, final_g):
    bsz, seq = x.shape[0], x.shape[1]
    lb_cum = jnp.cumsum(jax.nn.softmax(lb_logits.astype(jnp.float32), axis=0), axis=0)
    lower_bounds = lb_cum - lb_cum[0:1]
    c_act = jax.nn.silu(c)
    for layer in range(DEPTH):
        mod = c_act @ w_ada[layer] + b_ada[layer]
        shift, scale, gate = jnp.split(mod, 3, axis=-1)
        h = rms_norm(x, norm_g[layer]) * (1.0 + scale[:, None, :]) + shift[:, None, :]
        u = h @ w_in[layer]
        a_q, a_f, a_i, a_g, b_q, b_k, b_v, b_g = jnp.split(u, IN_OFFSETS, axis=-1)
        o_a = hgrn2_mixer(a_q, a_f, a_i, lower_bounds[layer])
        o_a = rms_norm(o_a, hgrn_norm_g[layer]).reshape(bsz, seq, D_MIX_A)
        o_a = (o_a * jax.nn.silu(a_g.astype(jnp.float32))).astype(x.dtype)
        o_b = stick_breaking_attention(b_q.reshape(bsz, seq, SB_HEADS, SB_HEAD_DIM),
                                       b_k.reshape(bsz, seq, SB_HEADS, SB_HEAD_DIM),
                                       b_v.reshape(bsz, seq, SB_HEADS, SB_HEAD_DIM))
        o_b = rms_norm(o_b, sb_norm_g[layer]).reshape(bsz, seq, D_MIX_B)
        o_b = (o_b * jax.nn.silu(b_g.astype(jnp.float32))).astype(x.dtype)
        y = jnp.concatenate([o_a, o_b], axis=-1) @ w_out[layer]
        x = x + gate[:, None, :] * y
    return rms_norm(x, final_g)
```

```python
import functools

import jax
import jax.numpy as jnp
import numpy as np
from jax import lax
from jax.experimental import pallas as pl
from jax.experimental.pallas import tpu as pltpu

F32 = jnp.float32
BF16 = jnp.bfloat16

EPS = 1e-6
LANES = 128
D_MODEL = 1024
HGRN_HEADS = 8
HGRN_DK = 128
SB_HEADS = 16
SB_DH = 64
D_IN = 8 * D_MODEL

VMEM_LIMIT = 56 * 1024 * 1024

IN_TM = 1024
IN_TN = 2048
OUT_TM = 512
HG_C = 128
HG_HB = 2
HG_SB = 1024
HG_LEVELS = (1, 2, 4, 8, 16, 32, 64)
AT_TQ = 256
AT_TK = 256
AT_HB = 2


def _dot(a, b):
    return jnp.dot(a, b, preferred_element_type=F32)


def _dot_nt(a, b):
    return lax.dot_general(a, b, (((1,), (1,)), ((), ())), preferred_element_type=F32)


def _dot_tn(a, b):
    return lax.dot_general(a, b, (((0,), (0,)), ((), ())), preferred_element_type=F32)


def _silu(x):
    return x / (1.0 + jnp.exp(-x))


def _split_bf16(x):
    hi = x.astype(BF16)
    lo = (x - hi.astype(F32)).astype(BF16)
    return hi, lo


def _mod_kernel(c_ref, w_ref, b_ref, o_ref):
    ca = _silu(c_ref[...])
    o_ref[0] = jnp.dot(ca, w_ref[0], preferred_element_type=F32,
                       precision=lax.Precision.HIGHEST) + b_ref[0]


def _adaln_mod(c, w_ada, b_ada):
    depth, d, d3 = w_ada.shape
    bsz = c.shape[0]
    tn = d
    return pl.pallas_call(
        _mod_kernel,
        grid=(depth, d3 // tn),
        in_specs=[
            pl.BlockSpec((bsz, d), lambda l, j: (0, 0)),
            pl.BlockSpec((1, d, tn), lambda l, j: (l, 0, j)),
            pl.BlockSpec((1, 1, tn), lambda l, j: (l, 0, j)),
        ],
        out_specs=pl.BlockSpec((1, bsz, tn), lambda l, j: (l, 0, j)),
        out_shape=jax.ShapeDtypeStruct((depth, bsz, d3), F32),
        compiler_params=pltpu.CompilerParams(
            dimension_semantics=("arbitrary", "arbitrary"), vmem_limit_bytes=VMEM_LIMIT),
        name="adaln_mod",
    )(c, w_ada, b_ada.reshape(depth, 1, d3))


def _inproj_kernel(x_ref, mod_ref, g_ref, w_ref, o_ref, h_ref):
    d = x_ref.shape[1]

    @pl.when(pl.program_id(1) == 0)
    def _():
        x = x_ref[...]
        ms = jnp.mean(x * x, axis=-1, keepdims=True)
        y = x * lax.rsqrt(ms + EPS) * g_ref[...]
        shift = mod_ref[0, :, 0:d]
        scale = mod_ref[0, :, d:2 * d]
        h_ref[...] = (y * (1.0 + scale) + shift).astype(BF16)

    o_ref[...] = _dot(h_ref[...], w_ref[...])


def _inproj(x2, mod_l, g, w_bf16, seq):
    t, d = x2.shape
    n = w_bf16.shape[1]
    bsz = mod_l.shape[0]
    steps_per_batch = seq // IN_TM
    return pl.pallas_call(
        _inproj_kernel,
        grid=(t // IN_TM, n // IN_TN),
        in_specs=[
            pl.BlockSpec((IN_TM, d), lambda i, j: (i, 0)),
            pl.BlockSpec((1, 1, 3 * d), lambda i, j: (i // steps_per_batch, 0, 0)),
            pl.BlockSpec((1, d), lambda i, j: (0, 0)),
            pl.BlockSpec((d, IN_TN), lambda i, j: (0, j)),
        ],
        out_specs=pl.BlockSpec((IN_TM, IN_TN), lambda i, j: (i, j)),
        out_shape=jax.ShapeDtypeStruct((t, n), F32),
        scratch_shapes=[pltpu.VMEM((IN_TM, d), BF16)],
        compiler_params=pltpu.CompilerParams(
            dimension_semantics=("parallel", "arbitrary"), vmem_limit_bytes=VMEM_LIMIT),
        name="inproj",
    )(x2, mod_l.reshape(bsz, 1, 3 * d), g.reshape(1, d), w_bf16)


def _hgrn_tables():
    c = HG_C
    t = np.arange(c)[:, None]
    j = np.arange(c)[None, :]
    blocks = [(j <= t)]
    for h in HG_LEVELS:
        mid = (t // (2 * h)) * 2 * h + h - 1
        blocks.append(((j > mid) & (j <= t)) | ((j > t) & (j <= mid)))
    pmat = np.concatenate(blocks, axis=0).astype(np.float32)
    x = t ^ j
    lv = np.where(j > t, -1, np.where(j == t, 0, 1 + np.floor(np.log2(np.maximum(x, 1))))).astype(np.int32)
    return jnp.asarray(pmat, BF16), jnp.asarray(lv)


def _hgrn_kernel(layer, q_ref, f_ref, i_ref, g_ref, lbl_ref, gn_ref, pmat_ref, lv_ref,
                 o_ref, st_ref, d_ref):
    c = HG_C

    @pl.when(pl.program_id(2) == 0)
    def _():
        st_ref[...] = jnp.zeros_like(st_ref)

    if layer > 0:
        lg = lbl_ref[...]
        e = jnp.exp(lg - jnp.max(lg, axis=0, keepdims=True))
        p = e / jnp.sum(e, axis=0, keepdims=True)
        lb = p[1:2, :]
        for l in range(2, layer + 1):
            lb = lb + p[l:l + 1, :]
        log_lb = jnp.log(lb)
        log1m_lb = jnp.log(1.0 - lb)
        one_m_lb = 1.0 - lb

    gn = gn_ref[...]

    def chunk(ci, carry):
        r0 = pl.multiple_of(ci * c, c)
        rows = pl.ds(r0, c)
        fl = f_ref[rows, :]
        e = jnp.exp(-jnp.abs(fl))
        ope = 1.0 + e
        ls = jnp.minimum(fl, 0.0) - jnp.log(ope)
        k_all = jnp.where(fl >= 0.0, e, 1.0) / ope
        if layer == 0:
            lf = ls
        else:
            bb = log1m_lb + ls
            lf = jnp.maximum(log_lb, bb) + jnp.log(1.0 + jnp.exp(-jnp.abs(log_lb - bb)))
            k_all = one_m_lb * k_all
        hi, lo = _split_bf16(lf)
        pm = pmat_ref[...]
        d_ref[...] = _dot(pm, hi) + _dot(pm, lo)

        lv = lv_ref[...]
        for hd in range(HG_HB):
            sl = slice(hd * LANES, (hd + 1) * LANES)
            qs = _silu(q_ref[rows, sl])
            kk = k_all[:, sl]
            iv = i_ref[rows, sl].astype(BF16)
            b = d_ref[0:c, sl]
            sc = jnp.where(lv == 0, _dot_nt(qs.astype(BF16), kk.astype(BF16)), 0.0)
            for idx in range(1, len(HG_LEVELS) + 1):
                dec = jnp.exp(d_ref[idx * c:(idx + 1) * c, sl])
                s_h = _dot_nt((qs * dec).astype(BF16), (kk * dec).astype(BF16))
                sc = jnp.where(lv == idx, s_h, sc)
            o = _dot(sc.astype(BF16), iv)
            st = st_ref[hd]
            o = o + _dot_nt((qs * jnp.exp(b)).astype(BF16), st.astype(BF16))
            b_last = b[c - 1:c, :]
            k_dec = (kk * jnp.exp(b_last - b)).astype(BF16)
            st_ref[hd] = st * jnp.exp(b_last) + _dot_tn(iv, k_dec)
            ms = jnp.mean(o * o, axis=-1, keepdims=True)
            y = o * lax.rsqrt(ms + EPS) * gn * _silu(g_ref[rows, sl])
            o_ref[rows, sl] = y.astype(o_ref.dtype)
        return carry

    lax.fori_loop(0, HG_SB // c, chunk, 0)


def _hgrn(u, lb_logits, gn, layer, bsz, seq):
    t = u.shape[0]
    w = HG_HB * LANES
    nsb = seq // HG_SB
    depth = lb_logits.shape[0]
    pmat, lv = _hgrn_tables()
    hpb = D_MODEL // w

    def col(section):
        return lambda b, h, s: (b * nsb + s, section * hpb + h)

    return pl.pallas_call(
        functools.partial(_hgrn_kernel, layer),
        grid=(bsz, HGRN_HEADS // HG_HB, nsb),
        in_specs=[
            pl.BlockSpec((HG_SB, w), col(0)),
            pl.BlockSpec((HG_SB, w), col(1)),
            pl.BlockSpec((HG_SB, w), col(2)),
            pl.BlockSpec((HG_SB, w), col(3)),
            pl.BlockSpec((depth, w), lambda b, h, s: (0, h)),
            pl.BlockSpec((1, LANES), lambda b, h, s: (0, 0)),
            pl.BlockSpec(pmat.shape, lambda b, h, s: (0, 0)),
            pl.BlockSpec(lv.shape, lambda b, h, s: (0, 0)),
        ],
        out_specs=pl.BlockSpec((HG_SB, w), lambda b, h, s: (b * nsb + s, h)),
        out_shape=jax.ShapeDtypeStruct((t, D_MODEL), BF16),
        scratch_shapes=[
            pltpu.VMEM((HG_HB, LANES, LANES), F32),
            pltpu.VMEM((pmat.shape[0], w), F32),
        ],
        compiler_params=pltpu.CompilerParams(
            dimension_semantics=("parallel", "parallel", "arbitrary"),
            vmem_limit_bytes=VMEM_LIMIT),
        name="hgrn2",
    )(u, u, u, u, lb_logits, gn.reshape(1, LANES), pmat, lv)


def _attn_tables():
    j = np.arange(AT_TK)
    after = (j[:, None] > j[None, :]).astype(np.float32)
    lane = np.arange(LANES)
    seg = (lane[:, None] // SB_DH == lane[None, :] // SB_DH).astype(np.float32) / SB_DH
    return jnp.asarray(after, BF16), jnp.asarray(seg, BF16)


def _attn_kernel(q_ref, k_ref, v_ref, g_ref, gn_ref, u_ref, seg_ref, o_ref,
                 kb_ref, vm_ref, acc_ref, car_ref):
    qi = pl.program_id(2)
    lane = lax.broadcasted_iota(jnp.int32, (1, LANES), 1)
    head_lanes = [(lane >= hd * SB_DH) & (lane < (hd + 1) * SB_DH) for hd in range(AT_HB)]

    @pl.when(qi == 0)
    def _():
        kb_ref[...] = k_ref[...].astype(BF16)
        v = v_ref[...]
        for hd in range(AT_HB):
            vm_ref[hd] = jnp.where(head_lanes[hd], v, 0.0).astype(BF16)

    q = q_ref[...] * (SB_DH ** -0.5)
    qm = [jnp.where(head_lanes[hd], q, 0.0).astype(BF16) for hd in range(AT_HB)]
    acc_ref[...] = jnp.zeros_like(acc_ref)
    car_ref[...] = jnp.zeros_like(car_ref)
    umat = u_ref[...]
    row = lax.broadcasted_iota(jnp.int32, (AT_TQ, AT_TK), 0)
    colm = lax.broadcasted_iota(jnp.int32, (AT_TQ, AT_TK), 1)
    causal = colm < row

    def tile(j, masked):
        c0 = pl.multiple_of(j * AT_TK, AT_TK)
        k_blk = kb_ref[pl.ds(c0, AT_TK), :]
        for hd in range(AT_HB):
            z = _dot_nt(qm[hd], k_blk)
            e = jnp.exp(-jnp.abs(z))
            sp_raw = jnp.maximum(z, 0.0) + jnp.log(1.0 + e)
            sp = jnp.where(causal, sp_raw, 0.0) if masked else sp_raw
            hi, lo = _split_bf16(sp)
            aft = _dot(hi, umat) + _dot(lo, umat)
            car = car_ref[hd]
            w = jnp.exp(z - sp_raw - aft - car)
            if masked:
                w = jnp.where(causal, w, 0.0)
            acc_ref[...] += _dot(w.astype(BF16), vm_ref[hd, pl.ds(c0, AT_TK), :])
            car_ref[hd] = car + aft[:, 0:1] + sp[:, 0:1]

    tile(qi, True)

    def body(it, carry):
        tile(qi - 1 - it, False)
        return carry

    lax.fori_loop(0, qi, body, 0)

    o = acc_ref[...]
    hi, lo = _split_bf16(o * o)
    seg = seg_ref[...]
    ms = _dot(hi, seg) + _dot(lo, seg)
    y = o * lax.rsqrt(ms + EPS) * gn_ref[...] * _silu(g_ref[...])
    o_ref[...] = y.astype(o_ref.dtype)


def _attn(u, gn, bsz, seq):
    t = u.shape[0]
    w = AT_HB * SB_DH
    nq = seq // AT_TQ
    umat, seg = _attn_tables()
    hpb = D_MODEL // w
    gn2 = jnp.tile(gn.reshape(1, SB_DH), (1, AT_HB))

    return pl.pallas_call(
        _attn_kernel,
        grid=(bsz, SB_HEADS // AT_HB, nq),
        in_specs=[
            pl.BlockSpec((AT_TQ, w), lambda b, h, i: (b * nq + i, 4 * hpb + h)),
            pl.BlockSpec((seq, w), lambda b, h, i: (b, 5 * hpb + h)),
            pl.BlockSpec((seq, w), lambda b, h, i: (b, 6 * hpb + h)),
            pl.BlockSpec((AT_TQ, w), lambda b, h, i: (b * nq + i, 7 * hpb + h)),
            pl.BlockSpec((1, w), lambda b, h, i: (0, 0)),
            pl.BlockSpec(umat.shape, lambda b, h, i: (0, 0)),
            pl.BlockSpec(seg.shape, lambda b, h, i: (0, 0)),
        ],
        out_specs=pl.BlockSpec((AT_TQ, w), lambda b, h, i: (b * nq + i, h)),
        out_shape=jax.ShapeDtypeStruct((t, D_MODEL), BF16),
        scratch_shapes=[
            pltpu.VMEM((seq, w), BF16),
            pltpu.VMEM((AT_HB, seq, w), BF16),
            pltpu.VMEM((AT_TQ, w), F32),
            pltpu.VMEM((AT_HB, AT_TQ, 1), F32),
        ],
        compiler_params=pltpu.CompilerParams(
            dimension_semantics=("parallel", "parallel", "arbitrary"),
            vmem_limit_bytes=VMEM_LIMIT),
        name="stickbreak",
    )(u, u, u, u, gn2, umat, seg)


def _outproj_kernel(final, oa_ref, ob_ref, w_ref, x_ref, mod_ref, fg_ref, o_ref):
    d = x_ref.shape[1]
    da = oa_ref.shape[1]
    y = _dot(oa_ref[...], w_ref[0:da, :]) + _dot(ob_ref[...], w_ref[da:, :])
    gate = mod_ref[0, :, 2 * d:3 * d]
    xn = x_ref[...] + gate * y
    if final:
        ms = jnp.mean(xn * xn, axis=-1, keepdims=True)
        xn = xn * lax.rsqrt(ms + EPS) * fg_ref[...]
    o_ref[...] = xn


def _outproj(o_a, o_b, w_bf16, x2, mod_l, final_g, final, seq):
    t, d = x2.shape
    bsz = mod_l.shape[0]
    steps_per_batch = seq // OUT_TM
    return pl.pallas_call(
        functools.partial(_outproj_kernel, final),
        grid=(t // OUT_TM,),
        in_specs=[
            pl.BlockSpec((OUT_TM, o_a.shape[1]), lambda i: (i, 0)),
            pl.BlockSpec((OUT_TM, o_b.shape[1]), lambda i: (i, 0)),
            pl.BlockSpec(w_bf16.shape, lambda i: (0, 0)),
            pl.BlockSpec((OUT_TM, d), lambda i: (i, 0)),
            pl.BlockSpec((1, 1, 3 * d), lambda i: (i // steps_per_batch, 0, 0)),
            pl.BlockSpec((1, d), lambda i: (0, 0)),
        ],
        out_specs=pl.BlockSpec((OUT_TM, d), lambda i: (i, 0)),
        out_shape=jax.ShapeDtypeStruct((t, d), F32),
        compiler_params=pltpu.CompilerParams(
            dimension_semantics=("parallel",), vmem_limit_bytes=VMEM_LIMIT),
        name="outproj",
    )(o_a, o_b, w_bf16, x2, mod_l.reshape(bsz, 1, 3 * d), final_g.reshape(1, d))


def kernel(x, c, w_ada, b_ada, norm_g, w_in, lb_logits, hgrn_norm_g, sb_norm_g, w_out, final_g):
    bsz, seq, d = x.shape
    depth = w_ada.shape[0]
    assert d == D_MODEL and w_in.shape[2] == D_IN
    assert seq % max(IN_TM, HG_SB, AT_TQ, OUT_TM) == 0 and AT_TQ == AT_TK
    mod = _adaln_mod(c, w_ada, b_ada)
    x2 = x.reshape(bsz * seq, d)
    for layer in range(depth):
        u = _inproj(x2, mod[layer], norm_g[layer], w_in[layer].astype(BF16), seq)
        o_a = _hgrn(u, lb_logits, hgrn_norm_g[layer], layer, bsz, seq)
        o_b = _attn(u, sb_norm_g[layer], bsz, seq)
        x2 = _outproj(o_a, o_b, w_out[layer].astype(BF16), x2, mod[layer], final_g,
                      layer == depth - 1, seq)
    return x2.reshape(bsz, seq, d)
```

```python
import functools
import math

import jax
import jax.numpy as jnp
import numpy as np
from jax import lax
from jax.experimental import pallas as pl
from jax.experimental.pallas import tpu as pltpu

F32 = jnp.float32
BF16 = jnp.bfloat16
U32 = jnp.uint32

EPS = 1e-6
LOG2E = math.log2(math.e)
LANES = 128
D_MODEL = 1024
HGRN_HEADS = 8
HGRN_DK = 128
SB_HEADS = 16
SB_DH = 64
D_IN = 8 * D_MODEL

VMEM_LIMIT = 56 * 1024 * 1024

IN_TM = 1024
IN_TN = 2048
IN_F32_SECTIONS = (0, 1, 3, 7)
IN_BF16_SECTIONS = (2, 4, 5, 6)
OUT_TM = 512
HG_C = 128
HG_HB = 2
HG_SB = 1024
HG_UNROLL = 2
HG_MXU_LEVELS = (2, 4)
HG_VPU_LEVELS = (8, 16, 32, 64)
AT_TQ = 256
AT_TK = 256
AT_HB = 4


def _dot(a, b):
    return jnp.dot(a, b, preferred_element_type=F32)


def _dot_nt(a, b):
    return lax.dot_general(a, b, (((1,), (1,)), ((), ())), preferred_element_type=F32)


def _dot_tn(a, b):
    return lax.dot_general(a, b, (((0,), (0,)), ((), ())), preferred_element_type=F32)


def _silu(x):
    return x / (1.0 + jnp.exp(-x))


def _split_bf16(x):
    hi = x.astype(BF16)
    return hi, (x - hi.astype(F32)).astype(BF16)


def _mod_kernel(c_ref, w_ref, b_ref, o_ref):
    ca = _silu(c_ref[...])
    o_ref[0] = jnp.dot(ca, w_ref[0], preferred_element_type=F32,
                       precision=lax.Precision.HIGHEST) + b_ref[0]


def _adaln_mod(c, w_ada, b_ada):
    depth, d, d3 = w_ada.shape
    bsz = c.shape[0]
    tn = d
    return pl.pallas_call(
        _mod_kernel,
        grid=(depth, d3 // tn),
        in_specs=[
            pl.BlockSpec((bsz, d), lambda l, j: (0, 0)),
            pl.BlockSpec((1, d, tn), lambda l, j: (l, 0, j)),
            pl.BlockSpec((1, 1, tn), lambda l, j: (l, 0, j)),
        ],
        out_specs=pl.BlockSpec((1, bsz, tn), lambda l, j: (l, 0, j)),
        out_shape=jax.ShapeDtypeStruct((depth, bsz, d3), F32),
        compiler_params=pltpu.CompilerParams(
            dimension_semantics=("arbitrary", "arbitrary"), vmem_limit_bytes=VMEM_LIMIT),
        name="adaln_mod",
    )(c, w_ada, b_ada.reshape(depth, 1, d3))


def _inproj_kernel(n32, x_ref, mod_ref, g_ref, w_ref, o32_ref, o16_ref, h_ref):
    d = x_ref.shape[1]
    j = pl.program_id(1)

    @pl.when(j == 0)
    def _():
        x = x_ref[...]
        ms = jnp.mean(x * x, axis=-1, keepdims=True)
        y = x * lax.rsqrt(ms + EPS) * g_ref[...]
        shift = mod_ref[0, :, 0:d]
        scale = mod_ref[0, :, d:2 * d]
        h_ref[...] = (y * (1.0 + scale) + shift).astype(BF16)

    r = _dot(h_ref[...], w_ref[...])

    @pl.when(j < n32)
    def _():
        o32_ref[...] = r

    @pl.when(j >= n32)
    def _():
        o16_ref[...] = r.astype(BF16)


def _inproj(x2, mod_l, g, w_perm, seq):
    t, d = x2.shape
    bsz = mod_l.shape[0]
    steps_per_batch = seq // IN_TM
    w32 = len(IN_F32_SECTIONS) * D_MODEL
    w16 = len(IN_BF16_SECTIONS) * D_MODEL
    n32 = w32 // IN_TN
    n16 = w16 // IN_TN
    return pl.pallas_call(
        functools.partial(_inproj_kernel, n32),
        grid=(t // IN_TM, n32 + n16),
        in_specs=[
            pl.BlockSpec((IN_TM, d), lambda i, j: (i, 0)),
            pl.BlockSpec((1, 1, 3 * d), lambda i, j: (i // steps_per_batch, 0, 0)),
            pl.BlockSpec((1, d), lambda i, j: (0, 0)),
            pl.BlockSpec((d, IN_TN), lambda i, j: (0, j)),
        ],
        out_specs=[
            pl.BlockSpec((IN_TM, IN_TN), lambda i, j: (i, jnp.minimum(j, n32 - 1))),
            pl.BlockSpec((IN_TM, IN_TN), lambda i, j: (i, jnp.maximum(j - n32, 0))),
        ],
        out_shape=[jax.ShapeDtypeStruct((t, w32), F32), jax.ShapeDtypeStruct((t, w16), BF16)],
        scratch_shapes=[pltpu.VMEM((IN_TM, d), BF16)],
        compiler_params=pltpu.CompilerParams(
            dimension_semantics=("parallel", "arbitrary"), vmem_limit_bytes=VMEM_LIMIT),
        name="inproj",
    )(x2, mod_l.reshape(bsz, 1, 3 * d), g.reshape(1, d), w_perm)


def _permute_w_in(w):
    sec = [w[:, s * D_MODEL:(s + 1) * D_MODEL] for s in range(8)]
    sec[4] = sec[4] * (SB_DH ** -0.5 * LOG2E)
    cols = [sec[s] for s in IN_F32_SECTIONS + IN_BF16_SECTIONS]
    return jnp.concatenate(cols, axis=1).astype(BF16)


def _hgrn_tables():
    c = HG_C
    t = np.arange(c)[:, None]
    j = np.arange(c)[None, :]
    blocks = [(j <= t)]
    for h in HG_MXU_LEVELS:
        mid = (t // (2 * h)) * 2 * h + h - 1
        blocks.append(((j > mid) & (j <= t)) | ((j > t) & (j <= mid)))
    pmat = np.concatenate(blocks, axis=0).astype(np.float32)
    x = t ^ j
    lv = np.where(j > t, -1, np.where(j == t, 0, 1 + np.floor(np.log2(np.maximum(x, 1))))).astype(np.int32)
    return jnp.asarray(pmat, BF16), jnp.asarray(lv)


def _hgrn_kernel(layer, q_ref, f_ref, i_ref, g_ref, lbl_ref, gn_ref, pmat_ref, lv_ref,
                 o_ref, st_ref, d_ref):
    c = HG_C
    heads = range(HG_HB)

    @pl.when(pl.program_id(2) == 0)
    def _():
        st_ref[...] = jnp.zeros_like(st_ref)

    if layer > 0:
        lg = lbl_ref[...]
        e = jnp.exp(lg - jnp.max(lg, axis=0, keepdims=True))
        p = e / jnp.sum(e, axis=0, keepdims=True)
        lb = p[1:2, :]
        for l in range(2, layer + 1):
            lb = lb + p[l:l + 1, :]
        log_lb = jnp.log(lb)
        log1m_lb = jnp.log(1.0 - lb)
        one_m_lb = 1.0 - lb

    gn = gn_ref[...]
    odd_row = (lax.broadcasted_iota(jnp.int32, (c, LANES), 0) & 1) == 1

    w = HG_HB * LANES

    def gates(rows):
        fl = f_ref[rows, :]
        e = jnp.exp(-jnp.abs(fl))
        ope = 1.0 + e
        ls = jnp.minimum(fl, 0.0) - jnp.log(ope)
        k_all = jnp.where(fl >= 0.0, e, 1.0) / ope
        if layer == 0:
            return ls, k_all
        bb = log1m_lb + ls
        lf = jnp.maximum(log_lb, bb) + jnp.log(1.0 + jnp.exp(-jnp.abs(log_lb - bb)))
        return lf, one_m_lb * k_all

    def chunk_group(gi, carry):
        chunks = range(HG_UNROLL)
        rows = [pl.ds(pl.multiple_of((gi * HG_UNROLL + t) * c, c), c) for t in chunks]
        lfk = [gates(rows[t]) for t in chunks]
        splits = [_split_bf16(lfk[t][0]) for t in chunks]
        pm = pmat_ref[...]
        d_ref[...] = (_dot(pm, jnp.concatenate([s[0] for s in splits], axis=1))
                      + _dot(pm, jnp.concatenate([s[1] for s in splits], axis=1)))

        lv = lv_ref[...]
        streams = [(t, hd) for t in chunks for hd in heads]
        sl_in = {s: slice(s[1] * LANES, (s[1] + 1) * LANES) for s in streams}
        sl_d = {s: slice(s[0] * w + s[1] * LANES, s[0] * w + (s[1] + 1) * LANES) for s in streams}
        qs = {s: _silu(q_ref[rows[s[0]], sl_in[s]]) for s in streams}
        kk = {s: lfk[s[0]][1][:, sl_in[s]] for s in streams}
        iv = {s: i_ref[rows[s[0]], sl_in[s]] for s in streams}
        bs = {s: d_ref[0:c, sl_d[s]] for s in streams}

        sc = {s: jnp.where(lv == 0, _dot_nt(qs[s].astype(BF16), kk[s].astype(BF16)), 0.0)
              for s in streams}

        def add_level(idx, dec_of):
            decs = {s: dec_of(s) for s in streams}
            for s in streams:
                s_h = _dot_nt((qs[s] * decs[s]).astype(BF16), (kk[s] * decs[s]).astype(BF16))
                sc[s] = jnp.where(lv == idx, s_h, sc[s])

        add_level(1, lambda s: jnp.where(odd_row, jnp.exp(lfk[s[0]][0][:, sl_in[s]]), 1.0))
        for n, h in enumerate(HG_MXU_LEVELS):
            add_level(1 + int(math.log2(h)),
                      lambda s, n=n: jnp.exp(d_ref[(n + 1) * c:(n + 2) * c, sl_d[s]]))
        for h in HG_VPU_LEVELS:
            def dec_of(s, h=h):
                b3 = bs[s].reshape(c // (2 * h), 2 * h, LANES)
                return jnp.exp(-jnp.abs(b3 - b3[:, h - 1:h, :])).reshape(c, LANES)
            add_level(1 + int(math.log2(h)), dec_of)

        o_intra = {s: _dot(sc[s].astype(BF16), iv[s]) for s in streams}
        o_inter = {}
        for hd in heads:
            st = st_ref[hd]
            for t in chunks:
                s = (t, hd)
                o_inter[s] = _dot_nt((qs[s] * jnp.exp(bs[s])).astype(BF16), st.astype(BF16))
                b_last = bs[s][c - 1:c, :]
                k_dec = (kk[s] * jnp.exp(b_last - bs[s])).astype(BF16)
                st = st * jnp.exp(b_last) + _dot_tn(iv[s], k_dec)
            st_ref[hd] = st
        for s in streams:
            o = o_intra[s] + o_inter[s]
            ms = jnp.mean(o * o, axis=-1, keepdims=True)
            y = o * lax.rsqrt(ms + EPS) * gn * _silu(g_ref[rows[s[0]], sl_in[s]])
            o_ref[rows[s[0]], sl_in[s]] = y.astype(o_ref.dtype)
        return carry

    lax.fori_loop(0, HG_SB // (c * HG_UNROLL), chunk_group, 0)


def _hgrn(u32, u16, lb_logits, gn, layer, bsz, seq):
    t = u32.shape[0]
    w = HG_HB * LANES
    nsb = seq // HG_SB
    depth = lb_logits.shape[0]
    pmat, lv = _hgrn_tables()
    hpb = D_MODEL // w

    def col(section):
        return lambda b, h, s: (b * nsb + s, section * hpb + h)

    return pl.pallas_call(
        functools.partial(_hgrn_kernel, layer),
        grid=(bsz, HGRN_HEADS // HG_HB, nsb),
        in_specs=[
            pl.BlockSpec((HG_SB, w), col(0)),
            pl.BlockSpec((HG_SB, w), col(1)),
            pl.BlockSpec((HG_SB, w), col(0)),
            pl.BlockSpec((HG_SB, w), col(2)),
            pl.BlockSpec((depth, w), lambda b, h, s: (0, h)),
            pl.BlockSpec((1, LANES), lambda b, h, s: (0, 0)),
            pl.BlockSpec(pmat.shape, lambda b, h, s: (0, 0)),
            pl.BlockSpec(lv.shape, lambda b, h, s: (0, 0)),
        ],
        out_specs=pl.BlockSpec((HG_SB, w), lambda b, h, s: (b * nsb + s, h)),
        out_shape=jax.ShapeDtypeStruct((t, D_MODEL), BF16),
        scratch_shapes=[
            pltpu.VMEM((HG_HB, LANES, LANES), F32),
            pltpu.VMEM((pmat.shape[0], HG_UNROLL * w), F32),
        ],
        compiler_params=pltpu.CompilerParams(
            dimension_semantics=("parallel", "parallel", "arbitrary"),
            vmem_limit_bytes=VMEM_LIMIT),
        name="hgrn2",
    )(u32, u32, u16, u32, lb_logits, gn.reshape(1, LANES), pmat, lv)


def _attn_tables():
    j = np.arange(AT_TK)
    after = (j[:, None] >= j[None, :]).astype(np.float32)
    lane = np.arange(AT_HB * SB_DH)
    seg = (lane[:, None] // SB_DH == lane[None, :] // SB_DH).astype(np.float32) / SB_DH
    return jnp.asarray(after, BF16), jnp.asarray(seg, BF16)


def _attn_kernel(q_ref, k_ref, v_ref, g_ref, gn_ref, u_ref, seg_ref, o_ref,
                 vm_ref, acc_ref, car_ref):
    qi = pl.program_id(2)
    heads = range(AT_HB)
    w = AT_HB * SB_DH
    lane = lax.broadcasted_iota(jnp.int32, (1, w), 1)
    head_lanes = [(lane >= hd * SB_DH) & (lane < (hd + 1) * SB_DH) for hd in heads]

    @pl.when(qi == 0)
    def _():
        v = v_ref[...]
        for hd in heads:
            vm_ref[hd] = jnp.where(head_lanes[hd], v, jnp.zeros_like(v))

    q = q_ref[...]
    qm = [jnp.where(head_lanes[hd], q, jnp.zeros_like(q)) for hd in heads]
    acc_ref[...] = jnp.zeros_like(acc_ref)
    car_ref[...] = jnp.zeros_like(car_ref)
    umat = u_ref[...]
    row = lax.broadcasted_iota(jnp.int32, (AT_TQ, AT_TK), 0)
    colm = lax.broadcasted_iota(jnp.int32, (AT_TQ, AT_TK), 1)
    causal = colm < row

    def tiles(js, masked):
        keys = [pl.ds(pl.multiple_of(j * AT_TK, AT_TK), AT_TK) for j in js]
        streams = [(t, hd) for t in range(len(js)) for hd in heads]
        kb = [k_ref[ks, :] for ks in keys]
        zs = {s: _dot_nt(qm[s[1]], kb[s[0]]) for s in streams}
        parts = {}
        for s in streams:
            z = zs[s]
            neg_abs = pltpu.bitcast(pltpu.bitcast(z, U32) | jnp.uint32(0x80000000), F32)
            sp = jnp.maximum(z, 0.0) + jnp.log(1.0 + jnp.exp2(neg_abs)) * LOG2E
            if masked:
                sp = jnp.where(causal, sp, 0.0)
            parts[s] = _split_bf16(sp)
        rs = {s: _dot(parts[s][0], umat) + _dot(parts[s][1], umat) for s in streams}
        pv = None
        for hd in heads:
            car = car_ref[hd]
            for t in range(len(js)):
                s = (t, hd)
                wgt = jnp.exp2(zs[s] - car - rs[s])
                if masked:
                    wgt = jnp.where(causal, wgt, 0.0)
                d = _dot(wgt.astype(BF16), vm_ref[hd, keys[t], :])
                pv = d if pv is None else pv + d
                car = car + rs[s][:, 0:1]
            car_ref[hd] = car
        acc_ref[...] += pv

    tiles([qi], True)
    n_off = qi
    odd = n_off & 1

    @pl.when(odd == 1)
    def _():
        tiles([qi - 1], False)

    first = qi - 1 - odd

    def body(it, carry):
        j = first - 2 * it
        tiles([j, j - 1], False)
        return carry

    lax.fori_loop(0, lax.shift_right_logical(n_off, 1), body, 0)

    o = acc_ref[...]
    hi, lo = _split_bf16(o * o)
    seg = seg_ref[...]
    ms = _dot(hi, seg) + _dot(lo, seg)
    y = o * lax.rsqrt(ms + EPS) * gn_ref[...] * _silu(g_ref[...])
    o_ref[...] = y.astype(o_ref.dtype)


def _attn(u32, u16, gn, bsz, seq):
    t = u32.shape[0]
    w = AT_HB * SB_DH
    nq = seq // AT_TQ
    umat, seg = _attn_tables()
    hpb = D_MODEL // w
    gn2 = jnp.tile(gn.reshape(1, SB_DH), (1, AT_HB))

    return pl.pallas_call(
        _attn_kernel,
        grid=(bsz, SB_HEADS // AT_HB, nq),
        in_specs=[
            pl.BlockSpec((AT_TQ, w), lambda b, h, i: (b * nq + i, 1 * hpb + h)),
            pl.BlockSpec((seq, w), lambda b, h, i: (b, 2 * hpb + h)),
            pl.BlockSpec((seq, w), lambda b, h, i: (b, 3 * hpb + h)),
            pl.BlockSpec((AT_TQ, w), lambda b, h, i: (b * nq + i, 3 * hpb + h)),
            pl.BlockSpec((1, w), lambda b, h, i: (0, 0)),
            pl.BlockSpec(umat.shape, lambda b, h, i: (0, 0)),
            pl.BlockSpec(seg.shape, lambda b, h, i: (0, 0)),
        ],
        out_specs=pl.BlockSpec((AT_TQ, w), lambda b, h, i: (b * nq + i, h)),
        out_shape=jax.ShapeDtypeStruct((t, D_MODEL), BF16),
        scratch_shapes=[
            pltpu.VMEM((AT_HB, seq, w), BF16),
            pltpu.VMEM((AT_TQ, w), F32),
            pltpu.VMEM((AT_HB, AT_TQ, 1), F32),
        ],
        compiler_params=pltpu.CompilerParams(
            dimension_semantics=("parallel", "parallel", "arbitrary"),
            vmem_limit_bytes=VMEM_LIMIT),
        name="stickbreak",
    )(u16, u16, u16, u32, gn2, umat, seg)


def _outproj_kernel(final, oa_ref, ob_ref, w_ref, x_ref, mod_ref, fg_ref, o_ref):
    d = x_ref.shape[1]
    da = oa_ref.shape[1]
    y = _dot(oa_ref[...], w_ref[0:da, :]) + _dot(ob_ref[...], w_ref[da:, :])
    gate = mod_ref[0, :, 2 * d:3 * d]
    xn = x_ref[...] + gate * y
    if final:
        ms = jnp.mean(xn * xn, axis=-1, keepdims=True)
        xn = xn * lax.rsqrt(ms + EPS) * fg_ref[...]
    o_ref[...] = xn


def _outproj(o_a, o_b, w_bf16, x2, mod_l, final_g, final, seq):
    t, d = x2.shape
    bsz = mod_l.shape[0]
    steps_per_batch = seq // OUT_TM
    return pl.pallas_call(
        functools.partial(_outproj_kernel, final),
        grid=(t // OUT_TM,),
        in_specs=[
            pl.BlockSpec((OUT_TM, o_a.shape[1]), lambda i: (i, 0)),
            pl.BlockSpec((OUT_TM, o_b.shape[1]), lambda i: (i, 0)),
            pl.BlockSpec(w_bf16.shape, lambda i: (0, 0)),
            pl.BlockSpec((OUT_TM, d), lambda i: (i, 0)),
            pl.BlockSpec((1, 1, 3 * d), lambda i: (i // steps_per_batch, 0, 0)),
            pl.BlockSpec((1, d), lambda i: (0, 0)),
        ],
        out_specs=pl.BlockSpec((OUT_TM, d), lambda i: (i, 0)),
        out_shape=jax.ShapeDtypeStruct((t, d), F32),
        compiler_params=pltpu.CompilerParams(
            dimension_semantics=("parallel",), vmem_limit_bytes=VMEM_LIMIT),
        name="outproj",
    )(o_a, o_b, w_bf16, x2, mod_l.reshape(bsz, 1, 3 * d), final_g.reshape(1, d))


def kernel(x, c, w_ada, b_ada, norm_g, w_in, lb_logits, hgrn_norm_g, sb_norm_g, w_out, final_g):
    bsz, seq, d = x.shape
    depth = w_ada.shape[0]
    assert d == D_MODEL and w_in.shape[2] == D_IN
    assert seq % max(IN_TM, HG_SB, AT_TQ, OUT_TM) == 0 and AT_TQ == AT_TK
    mod = _adaln_mod(c, w_ada, b_ada)
    x2 = x.reshape(bsz * seq, d)
    for layer in range(depth):
        u32, u16 = _inproj(x2, mod[layer], norm_g[layer], _permute_w_in(w_in[layer]), seq)
        o_a = _hgrn(u32, u16, lb_logits, hgrn_norm_g[layer], layer, bsz, seq)
        o_b = _attn(u32, u16, sb_norm_g[layer], bsz, seq)
        x2 = _outproj(o_a, o_b, w_out[layer].astype(BF16), x2, mod[layer], final_g,
                      layer == depth - 1, seq)
    return x2.reshape(bsz, seq, d)
```

```python
import functools
import math

import jax
import jax.numpy as jnp
import numpy as np
from jax import lax
from jax.experimental import pallas as pl
from jax.experimental.pallas import tpu as pltpu

F32 = jnp.float32
BF16 = jnp.bfloat16
U32 = jnp.uint32

EPS = 1e-6
LOG2E = math.log2(math.e)
LANES = 128
D_MODEL = 1024
HGRN_HEADS = 8
HGRN_DK = 128
SB_HEADS = 16
SB_DH = 64
D_IN = 8 * D_MODEL

VMEM_LIMIT = 56 * 1024 * 1024

IN_TM = 1024
IN_TN = 2048
IN_F32_SECTIONS = (0, 1, 3, 7)
IN_BF16_SECTIONS = (2, 4, 5, 6)
OUT_TM = 512
HG_C = 128
HG_HB = 2
HG_SB = 1024
HG_UNROLL = 2
HG_MXU_LEVELS = (2, 4)
HG_VPU_LEVELS = (8, 16, 32, 64)
AT_TQ = 256
AT_TK = 256
AT_HB = 4
AT_DEAD_LOG2 = 160.0


def _dot(a, b):
    return jnp.dot(a, b, preferred_element_type=F32)


def _dot_nt(a, b):
    return lax.dot_general(a, b, (((1,), (1,)), ((), ())), preferred_element_type=F32)


def _dot_tn(a, b):
    return lax.dot_general(a, b, (((0,), (0,)), ((), ())), preferred_element_type=F32)


def _silu(x):
    return x / (1.0 + jnp.exp(-x))


def _split_bf16(x):
    hi = x.astype(BF16)
    return hi, (x - hi.astype(F32)).astype(BF16)


def _mod_kernel(c_ref, w_ref, b_ref, o_ref):
    ca = _silu(c_ref[...])
    o_ref[0] = jnp.dot(ca, w_ref[0], preferred_element_type=F32,
                       precision=lax.Precision.HIGHEST) + b_ref[0]


def _adaln_mod(c, w_ada, b_ada):
    depth, d, d3 = w_ada.shape
    bsz = c.shape[0]
    tn = d
    return pl.pallas_call(
        _mod_kernel,
        grid=(depth, d3 // tn),
        in_specs=[
            pl.BlockSpec((bsz, d), lambda l, j: (0, 0)),
            pl.BlockSpec((1, d, tn), lambda l, j: (l, 0, j)),
            pl.BlockSpec((1, 1, tn), lambda l, j: (l, 0, j)),
        ],
        out_specs=pl.BlockSpec((1, bsz, tn), lambda l, j: (l, 0, j)),
        out_shape=jax.ShapeDtypeStruct((depth, bsz, d3), F32),
        compiler_params=pltpu.CompilerParams(
            dimension_semantics=("arbitrary", "arbitrary"), vmem_limit_bytes=VMEM_LIMIT),
        name="adaln_mod",
    )(c, w_ada, b_ada.reshape(depth, 1, d3))


def _inproj_kernel(n32, x_ref, mod_ref, g_ref, w_ref, o32_ref, o16_ref, h_ref):
    d = x_ref.shape[1]
    j = pl.program_id(1)

    @pl.when(j == 0)
    def _():
        x = x_ref[...]
        ms = jnp.mean(x * x, axis=-1, keepdims=True)
        y = x * lax.rsqrt(ms + EPS) * g_ref[...]
        shift = mod_ref[0, :, 0:d]
        scale = mod_ref[0, :, d:2 * d]
        h_ref[...] = (y * (1.0 + scale) + shift).astype(BF16)

    r = _dot(h_ref[...], w_ref[...])

    @pl.when(j < n32)
    def _():
        o32_ref[...] = r

    @pl.when(j >= n32)
    def _():
        o16_ref[...] = r.astype(BF16)


def _inproj(x2, mod_l, g, w_perm, seq):
    t, d = x2.shape
    bsz = mod_l.shape[0]
    steps_per_batch = seq // IN_TM
    w32 = len(IN_F32_SECTIONS) * D_MODEL
    w16 = len(IN_BF16_SECTIONS) * D_MODEL
    n32 = w32 // IN_TN
    n16 = w16 // IN_TN
    return pl.pallas_call(
        functools.partial(_inproj_kernel, n32),
        grid=(t // IN_TM, n32 + n16),
        in_specs=[
            pl.BlockSpec((IN_TM, d), lambda i, j: (i, 0)),
            pl.BlockSpec((1, 1, 3 * d), lambda i, j: (i // steps_per_batch, 0, 0)),
            pl.BlockSpec((1, d), lambda i, j: (0, 0)),
            pl.BlockSpec((d, IN_TN), lambda i, j: (0, j)),
        ],
        out_specs=[
            pl.BlockSpec((IN_TM, IN_TN), lambda i, j: (i, jnp.minimum(j, n32 - 1))),
            pl.BlockSpec((IN_TM, IN_TN), lambda i, j: (i, jnp.maximum(j - n32, 0))),
        ],
        out_shape=[jax.ShapeDtypeStruct((t, w32), F32), jax.ShapeDtypeStruct((t, w16), BF16)],
        scratch_shapes=[pltpu.VMEM((IN_TM, d), BF16)],
        compiler_params=pltpu.CompilerParams(
            dimension_semantics=("parallel", "arbitrary"), vmem_limit_bytes=VMEM_LIMIT),
        name="inproj",
    )(x2, mod_l.reshape(bsz, 1, 3 * d), g.reshape(1, d), w_perm)


def _permute_w_in(w):
    sec = [w[:, s * D_MODEL:(s + 1) * D_MODEL] for s in range(8)]
    sec[4] = sec[4] * (SB_DH ** -0.5 * LOG2E)
    cols = [sec[s] for s in IN_F32_SECTIONS + IN_BF16_SECTIONS]
    return jnp.concatenate(cols, axis=1).astype(BF16)


def _hgrn_tables():
    c = HG_C
    t = np.arange(c)[:, None]
    j = np.arange(c)[None, :]
    blocks = [(j <= t)]
    for h in HG_MXU_LEVELS:
        mid = (t // (2 * h)) * 2 * h + h - 1
        blocks.append(((j > mid) & (j <= t)) | ((j > t) & (j <= mid)))
    pmat = np.concatenate(blocks, axis=0).astype(np.float32)
    x = t ^ j
    lv = np.where(j > t, -1, np.where(j == t, 0, 1 + np.floor(np.log2(np.maximum(x, 1))))).astype(np.int32)
    return jnp.asarray(pmat, BF16), jnp.asarray(lv)


def _hgrn_kernel(layer, q_ref, f_ref, i_ref, g_ref, lbl_ref, gn_ref, pmat_ref, lv_ref,
                 o_ref, st_ref, d_ref):
    c = HG_C
    heads = range(HG_HB)

    @pl.when(pl.program_id(2) == 0)
    def _():
        st_ref[...] = jnp.zeros_like(st_ref)

    if layer > 0:
        lg = lbl_ref[...]
        e = jnp.exp(lg - jnp.max(lg, axis=0, keepdims=True))
        p = e / jnp.sum(e, axis=0, keepdims=True)
        lb = p[1:2, :]
        for l in range(2, layer + 1):
            lb = lb + p[l:l + 1, :]
        log_lb = jnp.log(lb)
        log1m_lb = jnp.log(1.0 - lb)
        one_m_lb = 1.0 - lb

    gn = gn_ref[...]
    odd_row = (lax.broadcasted_iota(jnp.int32, (c, LANES), 0) & 1) == 1

    w = HG_HB * LANES

    def gates(rows):
        fl = f_ref[rows, :]
        e = jnp.exp(-jnp.abs(fl))
        ope = 1.0 + e
        ls = jnp.minimum(fl, 0.0) - jnp.log(ope)
        k_all = jnp.where(fl >= 0.0, e, 1.0) / ope
        if layer == 0:
            return ls, k_all
        bb = log1m_lb + ls
        lf = jnp.maximum(log_lb, bb) + jnp.log(1.0 + jnp.exp(-jnp.abs(log_lb - bb)))
        return lf, one_m_lb * k_all

    def chunk_group(gi, carry):
        chunks = range(HG_UNROLL)
        rows = [pl.ds(pl.multiple_of((gi * HG_UNROLL + t) * c, c), c) for t in chunks]
        lfk = [gates(rows[t]) for t in chunks]
        splits = [_split_bf16(lfk[t][0]) for t in chunks]
        pm = pmat_ref[...]
        d_ref[...] = (_dot(pm, jnp.concatenate([s[0] for s in splits], axis=1))
                      + _dot(pm, jnp.concatenate([s[1] for s in splits], axis=1)))

        lv = lv_ref[...]
        streams = [(t, hd) for t in chunks for hd in heads]
        sl_in = {s: slice(s[1] * LANES, (s[1] + 1) * LANES) for s in streams}
        sl_d = {s: slice(s[0] * w + s[1] * LANES, s[0] * w + (s[1] + 1) * LANES) for s in streams}
        qs = {s: _silu(q_ref[rows[s[0]], sl_in[s]]) for s in streams}
        kk = {s: lfk[s[0]][1][:, sl_in[s]] for s in streams}
        iv = {s: i_ref[rows[s[0]], sl_in[s]] for s in streams}
        bs = {s: d_ref[0:c, sl_d[s]] for s in streams}

        sc = {s: jnp.where(lv == 0, _dot_nt(qs[s].astype(BF16), kk[s].astype(BF16)), 0.0)
              for s in streams}

        def add_level(idx, dec_of):
            decs = {s: dec_of(s) for s in streams}
            for s in streams:
                s_h = _dot_nt((qs[s] * decs[s]).astype(BF16), (kk[s] * decs[s]).astype(BF16))
                sc[s] = jnp.where(lv == idx, s_h, sc[s])

        add_level(1, lambda s: jnp.where(odd_row, jnp.exp(lfk[s[0]][0][:, sl_in[s]]), 1.0))
        for n, h in enumerate(HG_MXU_LEVELS):
            add_level(1 + int(math.log2(h)),
                      lambda s, n=n: jnp.exp(d_ref[(n + 1) * c:(n + 2) * c, sl_d[s]]))
        for h in HG_VPU_LEVELS:
            def dec_of(s, h=h):
                b3 = bs[s].reshape(c // (2 * h), 2 * h, LANES)
                return jnp.exp(-jnp.abs(b3 - b3[:, h - 1:h, :])).reshape(c, LANES)
            add_level(1 + int(math.log2(h)), dec_of)

        o_intra = {s: _dot(sc[s].astype(BF16), iv[s]) for s in streams}
        o_inter = {}
        for hd in heads:
            st = st_ref[hd]
            for t in chunks:
                s = (t, hd)
                o_inter[s] = _dot_nt((qs[s] * jnp.exp(bs[s])).astype(BF16), st.astype(BF16))
                b_last = bs[s][c - 1:c, :]
                k_dec = (kk[s] * jnp.exp(b_last - bs[s])).astype(BF16)
                st = st * jnp.exp(b_last) + _dot_tn(iv[s], k_dec)
            st_ref[hd] = st
        for s in streams:
            o = o_intra[s] + o_inter[s]
            ms = jnp.mean(o * o, axis=-1, keepdims=True)
            y = o * lax.rsqrt(ms + EPS) * gn * _silu(g_ref[rows[s[0]], sl_in[s]])
            o_ref[rows[s[0]], sl_in[s]] = y.astype(o_ref.dtype)
        return carry

    lax.fori_loop(0, HG_SB // (c * HG_UNROLL), chunk_group, 0)


def _hgrn(u32, u16, lb_logits, gn, layer, bsz, seq):
    t = u32.shape[0]
    w = HG_HB * LANES
    nsb = seq // HG_SB
    depth = lb_logits.shape[0]
    pmat, lv = _hgrn_tables()
    hpb = D_MODEL // w

    def col(section):
        return lambda b, h, s: (b * nsb + s, section * hpb + h)

    return pl.pallas_call(
        functools.partial(_hgrn_kernel, layer),
        grid=(bsz, HGRN_HEADS // HG_HB, nsb),
        in_specs=[
            pl.BlockSpec((HG_SB, w), col(0)),
            pl.BlockSpec((HG_SB, w), col(1)),
            pl.BlockSpec((HG_SB, w), col(0)),
            pl.BlockSpec((HG_SB, w), col(2)),
            pl.BlockSpec((depth, w), lambda b, h, s: (0, h)),
            pl.BlockSpec((1, LANES), lambda b, h, s: (0, 0)),
            pl.BlockSpec(pmat.shape, lambda b, h, s: (0, 0)),
            pl.BlockSpec(lv.shape, lambda b, h, s: (0, 0)),
        ],
        out_specs=pl.BlockSpec((HG_SB, w), lambda b, h, s: (b * nsb + s, h)),
        out_shape=jax.ShapeDtypeStruct((t, D_MODEL), BF16),
        scratch_shapes=[
            pltpu.VMEM((HG_HB, LANES, LANES), F32),
            pltpu.VMEM((pmat.shape[0], HG_UNROLL * w), F32),
        ],
        compiler_params=pltpu.CompilerParams(
            dimension_semantics=("parallel", "parallel", "arbitrary"),
            vmem_limit_bytes=VMEM_LIMIT),
        name="hgrn2",
    )(u32, u32, u16, u32, lb_logits, gn.reshape(1, LANES), pmat, lv)


def _attn_tables():
    j = np.arange(AT_TK)
    after = (j[:, None] >= j[None, :]).astype(np.float32)
    lane = np.arange(AT_HB * SB_DH)
    seg = (lane[:, None] // SB_DH == lane[None, :] // SB_DH).astype(np.float32) / SB_DH
    return jnp.asarray(after, BF16), jnp.asarray(seg, BF16)


def _attn_kernel(q_ref, k_ref, v_ref, g_ref, gn_ref, u_ref, seg_ref, o_ref,
                 vm_ref, acc_ref, car_ref):
    qi = pl.program_id(2)
    heads = range(AT_HB)
    w = AT_HB * SB_DH
    lane = lax.broadcasted_iota(jnp.int32, (1, w), 1)
    head_lanes = [(lane >= hd * SB_DH) & (lane < (hd + 1) * SB_DH) for hd in heads]

    @pl.when(qi == 0)
    def _():
        v = v_ref[...]
        for hd in heads:
            vm_ref[hd] = jnp.where(head_lanes[hd], v, jnp.zeros_like(v))

    q = q_ref[...]
    qm = [jnp.where(head_lanes[hd], q, jnp.zeros_like(q)) for hd in heads]
    acc_ref[...] = jnp.zeros_like(acc_ref)
    car_ref[...] = jnp.zeros_like(car_ref)
    umat = u_ref[...]
    row = lax.broadcasted_iota(jnp.int32, (AT_TQ, AT_TK), 0)
    colm = lax.broadcasted_iota(jnp.int32, (AT_TQ, AT_TK), 1)
    causal = colm < row

    def tiles(js, masked):
        keys = [pl.ds(pl.multiple_of(j * AT_TK, AT_TK), AT_TK) for j in js]
        streams = [(t, hd) for t in range(len(js)) for hd in heads]
        kb = [k_ref[ks, :] for ks in keys]
        zs = {s: _dot_nt(qm[s[1]], kb[s[0]]) for s in streams}
        parts = {}
        for s in streams:
            z = zs[s]
            sp = jnp.maximum(z, 0.0) + jnp.log(1.0 + jnp.exp2(-jnp.abs(z))) * LOG2E
            if masked[s[0]]:
                sp = jnp.where(causal, sp, 0.0)
            parts[s] = _split_bf16(sp)
        rs = {s: _dot(parts[s][0], umat) + _dot(parts[s][1], umat) for s in streams}
        pv = None
        for hd in heads:
            car = car_ref[hd]
            for t in range(len(js)):
                s = (t, hd)
                wgt = jnp.exp2(zs[s] - car - rs[s])
                if masked[t]:
                    wgt = jnp.where(causal, wgt, 0.0)
                d = _dot(wgt.astype(BF16), vm_ref[hd, keys[t], :])
                pv = d if pv is None else pv + d
                car = car + rs[s][:, 0:1]
            car_ref[hd] = car
        acc_ref[...] += pv

    def all_underflowed():
        m = car_ref[0]
        for hd in range(1, AT_HB):
            m = jnp.minimum(m, car_ref[hd])
        return (jnp.min(m) >= AT_DEAD_LOG2).astype(jnp.int32)

    @pl.when(qi == 0)
    def _():
        tiles([qi], [True])

    @pl.when(qi > 0)
    def _():
        tiles([qi, qi - 1], [True, False])

    rem = jnp.maximum(qi - 1, 0)
    n_pairs = lax.shift_right_logical(rem, 1)

    def cond(state):
        it, dead = state
        return (it < n_pairs) & (dead == 0)

    def body(state):
        it, _ = state
        j = qi - 2 - 2 * it
        tiles([j, j - 1], [False, False])
        return it + 1, all_underflowed()

    _, dead = lax.while_loop(cond, body, (jnp.int32(0), all_underflowed()))

    @pl.when(((rem & 1) == 1) & (dead == 0))
    def _():
        tiles([0], [False])

    o = acc_ref[...]
    hi, lo = _split_bf16(o * o)
    seg = seg_ref[...]
    ms = _dot(hi, seg) + _dot(lo, seg)
    y = o * lax.rsqrt(ms + EPS) * gn_ref[...] * _silu(g_ref[...])
    o_ref[...] = y.astype(o_ref.dtype)


def _attn(u32, u16, gn, bsz, seq):
    t = u32.shape[0]
    w = AT_HB * SB_DH
    nq = seq // AT_TQ
    umat, seg = _attn_tables()
    hpb = D_MODEL // w
    gn2 = jnp.tile(gn.reshape(1, SB_DH), (1, AT_HB))

    return pl.pallas_call(
        _attn_kernel,
        grid=(bsz, SB_HEADS // AT_HB, nq),
        in_specs=[
            pl.BlockSpec((AT_TQ, w), lambda b, h, i: (b * nq + i, 1 * hpb + h)),
            pl.BlockSpec((seq, w), lambda b, h, i: (b, 2 * hpb + h)),
            pl.BlockSpec((seq, w), lambda b, h, i: (b, 3 * hpb + h)),
            pl.BlockSpec((AT_TQ, w), lambda b, h, i: (b * nq + i, 3 * hpb + h)),
            pl.BlockSpec((1, w), lambda b, h, i: (0, 0)),
            pl.BlockSpec(umat.shape, lambda b, h, i: (0, 0)),
            pl.BlockSpec(seg.shape, lambda b, h, i: (0, 0)),
        ],
        out_specs=pl.BlockSpec((AT_TQ, w), lambda b, h, i: (b * nq + i, h)),
        out_shape=jax.ShapeDtypeStruct((t, D_MODEL), BF16),
        scratch_shapes=[
            pltpu.VMEM((AT_HB, seq, w), BF16),
            pltpu.VMEM((AT_TQ, w), F32),
            pltpu.VMEM((AT_HB, AT_TQ, 1), F32),
        ],
        compiler_params=pltpu.CompilerParams(
            dimension_semantics=("parallel", "parallel", "arbitrary"),
            vmem_limit_bytes=VMEM_LIMIT),
        name="stickbreak",
    )(u16, u16, u16, u32, gn2, umat, seg)


def _outproj_kernel(final, oa_ref, ob_ref, w_ref, x_ref, mod_ref, fg_ref, o_ref):
    d = x_ref.shape[1]
    da = oa_ref.shape[1]
    y = _dot(oa_ref[...], w_ref[0:da, :]) + _dot(ob_ref[...], w_ref[da:, :])
    gate = mod_ref[0, :, 2 * d:3 * d]
    xn = x_ref[...] + gate * y
    if final:
        ms = jnp.mean(xn * xn, axis=-1, keepdims=True)
        xn = xn * lax.rsqrt(ms + EPS) * fg_ref[...]
    o_ref[...] = xn


def _outproj(o_a, o_b, w_bf16, x2, mod_l, final_g, final, seq):
    t, d = x2.shape
    bsz = mod_l.shape[0]
    steps_per_batch = seq // OUT_TM
    return pl.pallas_call(
        functools.partial(_outproj_kernel, final),
        grid=(t // OUT_TM,),
        in_specs=[
            pl.BlockSpec((OUT_TM, o_a.shape[1]), lambda i: (i, 0)),
            pl.BlockSpec((OUT_TM, o_b.shape[1]), lambda i: (i, 0)),
            pl.BlockSpec(w_bf16.shape, lambda i: (0, 0)),
            pl.BlockSpec((OUT_TM, d), lambda i: (i, 0)),
            pl.BlockSpec((1, 1, 3 * d), lambda i: (i // steps_per_batch, 0, 0)),
            pl.BlockSpec((1, d), lambda i: (0, 0)),
        ],
        out_specs=pl.BlockSpec((OUT_TM, d), lambda i: (i, 0)),
        out_shape=jax.ShapeDtypeStruct((t, d), F32),
        compiler_params=pltpu.CompilerParams(
            dimension_semantics=("parallel",), vmem_limit_bytes=VMEM_LIMIT),
        name="outproj",
    )(o_a, o_b, w_bf16, x2, mod_l.reshape(bsz, 1, 3 * d), final_g.reshape(1, d))


def kernel(x, c, w_ada, b_ada, norm_g, w_in, lb_logits, hgrn_norm_g, sb_norm_g, w_out, final_g):
    bsz, seq, d = x.shape
    depth = w_ada.shape[0]
    assert d == D_MODEL and w_in.shape[2] == D_IN
    assert seq % max(IN_TM, HG_SB, AT_TQ, OUT_TM) == 0 and AT_TQ == AT_TK
    mod = _adaln_mod(c, w_ada, b_ada)
    x2 = x.reshape(bsz * seq, d)
    for layer in range(depth):
        u32, u16 = _inproj(x2, mod[layer], norm_g[layer], _permute_w_in(w_in[layer]), seq)
        o_a = _hgrn(u32, u16, lb_logits, hgrn_norm_g[layer], layer, bsz, seq)
        o_b = _attn(u32, u16, sb_norm_g[layer], bsz, seq)
        x2 = _outproj(o_a, o_b, w_out[layer].astype(BF16), x2, mod[layer], final_g,
                      layer == depth - 1, seq)
    return x2.reshape(bsz, seq, d)
```

```python
import functools
import math

import jax
import jax.numpy as jnp
import numpy as np
from jax import lax
from jax.experimental import pallas as pl
from jax.experimental.pallas import tpu as pltpu

F32 = jnp.float32
BF16 = jnp.bfloat16
U32 = jnp.uint32

EPS = 1e-6
LOG2E = math.log2(math.e)
LANES = 128
D_MODEL = 1024
HGRN_HEADS = 8
HGRN_DK = 128
SB_HEADS = 16
SB_DH = 64
D_IN = 8 * D_MODEL

VMEM_LIMIT = 56 * 1024 * 1024

IN_TM = 1024
IN_TN = 2048
IN_F32_SECTIONS = (0, 1, 3, 7)
IN_BF16_SECTIONS = (2, 4, 5, 6)
OUT_TM = 512
HG_C = 128
HG_HB = 2
HG_SB = 2048
HG_UNROLL = 2
HG_MXU_LEVELS = (2, 4)
HG_VPU_LEVELS = (8, 16, 32, 64)
HG_SUB = 16
HG_SAFE_LOG2 = 100.0
AT_TQ = 256
AT_TK = 256
AT_HB = 4
AT_DEAD_LOG2 = 160.0


def _dot(a, b):
    return jnp.dot(a, b, preferred_element_type=F32)


def _dot_nt(a, b):
    return lax.dot_general(a, b, (((1,), (1,)), ((), ())), preferred_element_type=F32)


def _dot_tn(a, b):
    return lax.dot_general(a, b, (((0,), (0,)), ((), ())), preferred_element_type=F32)


def _silu(x):
    return x / (1.0 + jnp.exp2(x * -LOG2E))


def _split_bf16(x):
    hi = x.astype(BF16)
    return hi, (x - hi.astype(F32)).astype(BF16)


def _mod_kernel(c_ref, w_ref, b_ref, o_ref):
    ca = _silu(c_ref[...])
    o_ref[0] = jnp.dot(ca, w_ref[0], preferred_element_type=F32,
                       precision=lax.Precision.HIGHEST) + b_ref[0]


def _adaln_mod(c, w_ada, b_ada):
    depth, d, d3 = w_ada.shape
    bsz = c.shape[0]
    tn = d
    return pl.pallas_call(
        _mod_kernel,
        grid=(depth, d3 // tn),
        in_specs=[
            pl.BlockSpec((bsz, d), lambda l, j: (0, 0)),
            pl.BlockSpec((1, d, tn), lambda l, j: (l, 0, j)),
            pl.BlockSpec((1, 1, tn), lambda l, j: (l, 0, j)),
        ],
        out_specs=pl.BlockSpec((1, bsz, tn), lambda l, j: (l, 0, j)),
        out_shape=jax.ShapeDtypeStruct((depth, bsz, d3), F32),
        compiler_params=pltpu.CompilerParams(
            dimension_semantics=("arbitrary", "arbitrary"), vmem_limit_bytes=VMEM_LIMIT),
        name="adaln_mod",
    )(c, w_ada, b_ada.reshape(depth, 1, d3))


def _inproj_kernel(n32, x_ref, mod_ref, g_ref, w_ref, o32_ref, o16_ref, h_ref):
    d = x_ref.shape[1]
    j = pl.program_id(1)

    @pl.when(j == 0)
    def _():
        x = x_ref[...]
        ms = jnp.mean(x * x, axis=-1, keepdims=True)
        y = x * lax.rsqrt(ms + EPS) * g_ref[...]
        shift = mod_ref[0, :, 0:d]
        scale = mod_ref[0, :, d:2 * d]
        h_ref[...] = (y * (1.0 + scale) + shift).astype(BF16)

    r = _dot(h_ref[...], w_ref[...])

    @pl.when(j < n32)
    def _():
        o32_ref[...] = r

    @pl.when(j >= n32)
    def _():
        o16_ref[...] = r.astype(BF16)


def _inproj(x2, mod_l, g, w_perm, seq):
    t, d = x2.shape
    bsz = mod_l.shape[0]
    steps_per_batch = seq // IN_TM
    w32 = len(IN_F32_SECTIONS) * D_MODEL
    w16 = len(IN_BF16_SECTIONS) * D_MODEL
    n32 = w32 // IN_TN
    n16 = w16 // IN_TN
    return pl.pallas_call(
        functools.partial(_inproj_kernel, n32),
        grid=(t // IN_TM, n32 + n16),
        in_specs=[
            pl.BlockSpec((IN_TM, d), lambda i, j: (i, 0)),
            pl.BlockSpec((1, 1, 3 * d), lambda i, j: (i // steps_per_batch, 0, 0)),
            pl.BlockSpec((1, d), lambda i, j: (0, 0)),
            pl.BlockSpec((d, IN_TN), lambda i, j: (0, j)),
        ],
        out_specs=[
            pl.BlockSpec((IN_TM, IN_TN), lambda i, j: (i, jnp.minimum(j, n32 - 1))),
            pl.BlockSpec((IN_TM, IN_TN), lambda i, j: (i, jnp.maximum(j - n32, 0))),
        ],
        out_shape=[jax.ShapeDtypeStruct((t, w32), F32), jax.ShapeDtypeStruct((t, w16), BF16)],
        scratch_shapes=[pltpu.VMEM((IN_TM, d), BF16)],
        compiler_params=pltpu.CompilerParams(
            dimension_semantics=("parallel", "arbitrary"), vmem_limit_bytes=VMEM_LIMIT),
        name="inproj",
    )(x2, mod_l.reshape(bsz, 1, 3 * d), g.reshape(1, d), w_perm)


def _permute_w_in(w):
    sec = [w[:, s * D_MODEL:(s + 1) * D_MODEL] for s in range(8)]
    sec[4] = sec[4] * (SB_DH ** -0.5 * LOG2E)
    cols = [sec[s] for s in IN_F32_SECTIONS + IN_BF16_SECTIONS]
    return jnp.concatenate(cols, axis=1).astype(BF16)


def _hgrn_tables():
    c = HG_C
    t = np.arange(c)[:, None]
    j = np.arange(c)[None, :]
    blocks = [(j <= t)]
    for h in HG_MXU_LEVELS:
        mid = (t // (2 * h)) * 2 * h + h - 1
        blocks.append(((j > mid) & (j <= t)) | ((j > t) & (j <= mid)))
    pmat = np.concatenate(blocks, axis=0).astype(np.float32)
    x = t ^ j
    lv = np.where(j > t, -1, np.where(j == t, 0, 1 + np.floor(np.log2(np.maximum(x, 1))))).astype(np.int32)
    return jnp.asarray(pmat, BF16), jnp.asarray(lv)


def _hgrn_kernel(layer, q_ref, f_ref, i_ref, g_ref, lbl_ref, gn_ref, pmat_ref, lv_ref,
                 o_ref, st_ref, d_ref, dec1_ref, kk_ref):
    c = HG_C
    heads = range(HG_HB)

    @pl.when(pl.program_id(2) == 0)
    def _():
        st_ref[...] = jnp.zeros_like(st_ref)

    if layer > 0:
        lg = lbl_ref[...]
        e = jnp.exp(lg - jnp.max(lg, axis=0, keepdims=True))
        p = e / jnp.sum(e, axis=0, keepdims=True)
        lb = p[1:2, :]
        for l in range(2, layer + 1):
            lb = lb + p[l:l + 1, :]
        log_lb = jnp.log(lb)
        log1m_lb = jnp.log(1.0 - lb)
        one_m_lb = 1.0 - lb

    gn = gn_ref[...]
    row_id = lax.broadcasted_iota(jnp.int32, (c, LANES), 0)
    half_sign = {h: jnp.where((row_id & h) != 0, 1.0, -1.0) for h in HG_VPU_LEVELS}

    w = HG_HB * LANES

    def gates(rows):
        fl = f_ref[rows, :]
        e = jnp.exp2(jnp.abs(fl) * -LOG2E)
        ope = 1.0 + e
        ls = jnp.minimum(fl, 0.0) - jnp.log(ope)
        k_all = jnp.where(fl >= 0.0, e, 1.0) / ope
        if layer == 0:
            return ls * LOG2E, k_all
        bb = log1m_lb + ls
        lf = jnp.maximum(log_lb, bb) + jnp.log(1.0 + jnp.exp(-jnp.abs(log_lb - bb)))
        return lf * LOG2E, one_m_lb * k_all

    chunks = range(HG_UNROLL)
    n_groups = HG_SB // (c * HG_UNROLL)
    odd_row_w = (lax.broadcasted_iota(jnp.int32, (c, w), 0) & 1) == 1

    def group_rows(gi):
        return [pl.ds(pl.multiple_of((gi * HG_UNROLL + t) * c, c), c) for t in chunks]

    def front(gi):
        rows = group_rows(gi)
        lfk = [gates(rows[t]) for t in chunks]
        for t in chunks:
            dec1_ref[t] = jnp.where(odd_row_w, jnp.exp2(lfk[t][0]), 1.0).astype(BF16)
            kk_ref[t] = lfk[t][1].astype(BF16)
        splits = [_split_bf16(lfk[t][0]) for t in chunks]
        pm = pmat_ref[...]
        d_ref[...] = (_dot(pm, jnp.concatenate([s[0] for s in splits], axis=1))
                      + _dot(pm, jnp.concatenate([s[1] for s in splits], axis=1)))
        b3 = d_ref[0:c, :].reshape(c // HG_SUB, HG_SUB, HG_UNROLL * w)
        last = b3[:, HG_SUB - 1:HG_SUB, :]
        prev = jnp.concatenate([jnp.zeros_like(last[0:1]), last[:-1]], axis=0)
        return (jnp.max(prev - last) < HG_SAFE_LOG2).astype(jnp.int32)

    def back(gi, direct_sub):
        rows = group_rows(gi)
        lv = lv_ref[...]
        streams = [(t, hd) for t in chunks for hd in heads]
        sl_in = {s: slice(s[1] * LANES, (s[1] + 1) * LANES) for s in streams}
        sl_d = {s: slice(s[0] * w + s[1] * LANES, s[0] * w + (s[1] + 1) * LANES) for s in streams}
        qs = {s: _silu(q_ref[rows[s[0]], sl_in[s]]).astype(BF16) for s in streams}
        kk = {s: kk_ref[s[0], :, sl_in[s]] for s in streams}
        iv = {s: i_ref[rows[s[0]], sl_in[s]] for s in streams}
        bs = {s: d_ref[0:c, sl_d[s]] for s in streams}

        def add_level(idx, dec_of):
            decs = {s: dec_of(s) for s in streams}
            for s in streams:
                s_h = _dot_nt(qs[s] * decs[s], kk[s] * decs[s])
                sc[s] = jnp.where(lv == idx, s_h, sc[s])

        def block_level(h):
            def dec_of(s):
                b3 = bs[s].reshape(c // (2 * h), 2 * h, LANES)
                d = (b3 - b3[:, h - 1:h, :]).reshape(c, LANES) * half_sign[h]
                return jnp.exp2(d).astype(BF16)
            add_level(1 + int(math.log2(h)), dec_of)

        if direct_sub:
            sub_ops = {}
            for s in streams:
                b3 = bs[s].reshape(c // HG_SUB, HG_SUB, LANES)
                last = b3[:, HG_SUB - 1:HG_SUB, :]
                start = jnp.concatenate([jnp.zeros_like(last[0:1]), last[:-1]], axis=0)
                dq = (b3 - start).reshape(c, LANES)
                sub_ops[s] = (qs[s] * jnp.exp2(dq).astype(BF16), kk[s] * jnp.exp2(-dq).astype(BF16))
            in_sub = (lv >= 0) & (lv <= int(math.log2(HG_SUB)))
            sc = {s: jnp.where(in_sub, _dot_nt(*sub_ops[s]), 0.0) for s in streams}
            for h in HG_VPU_LEVELS:
                if h >= HG_SUB:
                    block_level(h)
        else:
            sc = {s: jnp.where(lv == 0, _dot_nt(qs[s], kk[s]), 0.0) for s in streams}
            add_level(1, lambda s: dec1_ref[s[0], :, sl_in[s]])
            for n, h in enumerate(HG_MXU_LEVELS):
                add_level(1 + int(math.log2(h)),
                          lambda s, n=n: jnp.exp2(d_ref[(n + 1) * c:(n + 2) * c, sl_d[s]]).astype(BF16))
            for h in HG_VPU_LEVELS:
                block_level(h)

        o_intra = {s: _dot(sc[s].astype(BF16), iv[s]) for s in streams}
        q_dec = {s: qs[s] * jnp.exp2(bs[s]).astype(BF16) for s in streams}
        b_last = {s: bs[s][c - 1:c, :] for s in streams}
        k_dec = {s: kk[s] * jnp.exp2(b_last[s] - bs[s]).astype(BF16) for s in streams}

        next_safe = front(jnp.minimum(gi + 1, n_groups - 1))

        o_inter = {}
        for hd in heads:
            st = st_ref[hd]
            for t in chunks:
                s = (t, hd)
                o_inter[s] = _dot_nt(q_dec[s], st.astype(BF16))
                st = st * jnp.exp2(b_last[s]) + _dot_tn(iv[s], k_dec[s])
            st_ref[hd] = st
        for s in streams:
            o = o_intra[s] + o_inter[s]
            ms = jnp.mean(o * o, axis=-1, keepdims=True)
            y = o * lax.rsqrt(ms + EPS) * gn * _silu(g_ref[rows[s[0]], sl_in[s]])
            o_ref[rows[s[0]], sl_in[s]] = y.astype(o_ref.dtype)
        return next_safe

    def chunk_group(gi, safe):
        return lax.cond(safe == 1, lambda: back(gi, True), lambda: back(gi, False))

    lax.fori_loop(0, n_groups, chunk_group, front(0))


def _hgrn(u32, u16, lb_logits, gn, layer, bsz, seq):
    t = u32.shape[0]
    w = HG_HB * LANES
    nsb = seq // HG_SB
    depth = lb_logits.shape[0]
    pmat, lv = _hgrn_tables()
    hpb = D_MODEL // w

    def col(section):
        return lambda b, h, s: (b * nsb + s, section * hpb + h)

    return pl.pallas_call(
        functools.partial(_hgrn_kernel, layer),
        grid=(bsz, HGRN_HEADS // HG_HB, nsb),
        in_specs=[
            pl.BlockSpec((HG_SB, w), col(0)),
            pl.BlockSpec((HG_SB, w), col(1)),
            pl.BlockSpec((HG_SB, w), col(0)),
            pl.BlockSpec((HG_SB, w), col(2)),
            pl.BlockSpec((depth, w), lambda b, h, s: (0, h)),
            pl.BlockSpec((1, LANES), lambda b, h, s: (0, 0)),
            pl.BlockSpec(pmat.shape, lambda b, h, s: (0, 0)),
            pl.BlockSpec(lv.shape, lambda b, h, s: (0, 0)),
        ],
        out_specs=pl.BlockSpec((HG_SB, w), lambda b, h, s: (b * nsb + s, h)),
        out_shape=jax.ShapeDtypeStruct((t, D_MODEL), BF16),
        scratch_shapes=[
            pltpu.VMEM((HG_HB, LANES, LANES), F32),
            pltpu.VMEM((pmat.shape[0], HG_UNROLL * w), F32),
            pltpu.VMEM((HG_UNROLL, HG_C, w), BF16),
            pltpu.VMEM((HG_UNROLL, HG_C, w), BF16),
        ],
        compiler_params=pltpu.CompilerParams(
            dimension_semantics=("parallel", "parallel", "arbitrary"),
            vmem_limit_bytes=VMEM_LIMIT),
        name="hgrn2",
    )(u32, u32, u16, u32, lb_logits, gn.reshape(1, LANES), pmat, lv)


def _attn_tables():
    j = np.arange(AT_TK)
    after = (j[:, None] >= j[None, :]).astype(np.float32)
    lane = np.arange(AT_HB * SB_DH)
    seg = (lane[:, None] // SB_DH == lane[None, :] // SB_DH).astype(np.float32) / SB_DH
    return jnp.asarray(after, BF16), jnp.asarray(seg, BF16)


def _attn_kernel(q_ref, k_ref, v_ref, g_ref, gn_ref, u_ref, seg_ref, o_ref,
                 vm_ref, acc_ref, car_ref):
    qi = pl.program_id(2)
    heads = range(AT_HB)
    w = AT_HB * SB_DH
    lane = lax.broadcasted_iota(jnp.int32, (1, w), 1)
    head_lanes = [(lane >= hd * SB_DH) & (lane < (hd + 1) * SB_DH) for hd in heads]

    @pl.when(qi == 0)
    def _():
        v = v_ref[...]
        for hd in heads:
            vm_ref[hd] = jnp.where(head_lanes[hd], v, jnp.zeros_like(v))

    q = q_ref[...]
    qm = [jnp.where(head_lanes[hd], q, jnp.zeros_like(q)) for hd in heads]
    acc_ref[...] = jnp.zeros_like(acc_ref)
    car_ref[...] = jnp.zeros_like(car_ref)
    umat = u_ref[...]
    row = lax.broadcasted_iota(jnp.int32, (AT_TQ, AT_TK), 0)
    colm = lax.broadcasted_iota(jnp.int32, (AT_TQ, AT_TK), 1)
    causal = colm < row

    def tiles(js, masked):
        keys = [pl.ds(pl.multiple_of(j * AT_TK, AT_TK), AT_TK) for j in js]
        streams = [(t, hd) for t in range(len(js)) for hd in heads]
        kb = [k_ref[ks, :] for ks in keys]
        zs = {s: _dot_nt(qm[s[1]], kb[s[0]]) for s in streams}
        parts = {}
        for s in streams:
            z = zs[s]
            sp = jnp.maximum(z, 0.0) + jnp.log(1.0 + jnp.exp2(-jnp.abs(z))) * LOG2E
            if masked[s[0]]:
                sp = jnp.where(causal, sp, 0.0)
            parts[s] = _split_bf16(sp)
        rs = {s: _dot(parts[s][0], umat) + _dot(parts[s][1], umat) for s in streams}
        pv = None
        for hd in heads:
            car = car_ref[hd]
            for t in range(len(js)):
                s = (t, hd)
                wgt = jnp.exp2(zs[s] - car - rs[s])
                if masked[t]:
                    wgt = jnp.where(causal, wgt, 0.0)
                d = _dot(wgt.astype(BF16), vm_ref[hd, keys[t], :])
                pv = d if pv is None else pv + d
                car = car + rs[s][:, 0:1]
            car_ref[hd] = car
        acc_ref[...] += pv

    def all_underflowed():
        m = car_ref[0]
        for hd in range(1, AT_HB):
            m = jnp.minimum(m, car_ref[hd])
        return (jnp.min(m) >= AT_DEAD_LOG2).astype(jnp.int32)

    @pl.when(qi == 0)
    def _():
        tiles([qi], [True])

    @pl.when(qi > 0)
    def _():
        tiles([qi, qi - 1], [True, False])

    rem = jnp.maximum(qi - 1, 0)
    n_pairs = lax.shift_right_logical(rem, 1)

    def cond(state):
        it, dead = state
        return (it < n_pairs) & (dead == 0)

    def body(state):
        it, _ = state
        j = qi - 2 - 2 * it
        tiles([j, j - 1], [False, False])
        return it + 1, all_underflowed()

    _, dead = lax.while_loop(cond, body, (jnp.int32(0), all_underflowed()))

    @pl.when(((rem & 1) == 1) & (dead == 0))
    def _():
        tiles([0], [False])

    o = acc_ref[...]
    hi, lo = _split_bf16(o * o)
    seg = seg_ref[...]
    ms = _dot(hi, seg) + _dot(lo, seg)
    y = o * lax.rsqrt(ms + EPS) * gn_ref[...] * _silu(g_ref[...])
    o_ref[...] = y.astype(o_ref.dtype)


def _attn(u32, u16, gn, bsz, seq):
    t = u32.shape[0]
    w = AT_HB * SB_DH
    nq = seq // AT_TQ
    umat, seg = _attn_tables()
    hpb = D_MODEL // w
    gn2 = jnp.tile(gn.reshape(1, SB_DH), (1, AT_HB))

    return pl.pallas_call(
        _attn_kernel,
        grid=(bsz, SB_HEADS // AT_HB, nq),
        in_specs=[
            pl.BlockSpec((AT_TQ, w), lambda b, h, i: (b * nq + i, 1 * hpb + h)),
            pl.BlockSpec((seq, w), lambda b, h, i: (b, 2 * hpb + h)),
            pl.BlockSpec((seq, w), lambda b, h, i: (b, 3 * hpb + h)),
            pl.BlockSpec((AT_TQ, w), lambda b, h, i: (b * nq + i, 3 * hpb + h)),
            pl.BlockSpec((1, w), lambda b, h, i: (0, 0)),
            pl.BlockSpec(umat.shape, lambda b, h, i: (0, 0)),
            pl.BlockSpec(seg.shape, lambda b, h, i: (0, 0)),
        ],
        out_specs=pl.BlockSpec((AT_TQ, w), lambda b, h, i: (b * nq + i, h)),
        out_shape=jax.ShapeDtypeStruct((t, D_MODEL), BF16),
        scratch_shapes=[
            pltpu.VMEM((AT_HB, seq, w), BF16),
            pltpu.VMEM((AT_TQ, w), F32),
            pltpu.VMEM((AT_HB, AT_TQ, 1), F32),
        ],
        compiler_params=pltpu.CompilerParams(
            dimension_semantics=("parallel", "parallel", "arbitrary"),
            vmem_limit_bytes=VMEM_LIMIT),
        name="stickbreak",
    )(u16, u16, u16, u32, gn2, umat, seg)


def _outproj_kernel(final, oa_ref, ob_ref, w_ref, x_ref, mod_ref, fg_ref, o_ref):
    d = x_ref.shape[1]
    da = oa_ref.shape[1]
    y = _dot(oa_ref[...], w_ref[0:da, :]) + _dot(ob_ref[...], w_ref[da:, :])
    gate = mod_ref[0, :, 2 * d:3 * d]
    xn = x_ref[...] + gate * y
    if final:
        ms = jnp.mean(xn * xn, axis=-1, keepdims=True)
        xn = xn * lax.rsqrt(ms + EPS) * fg_ref[...]
    o_ref[...] = xn


def _outproj(o_a, o_b, w_bf16, x2, mod_l, final_g, final, seq):
    t, d = x2.shape
    bsz = mod_l.shape[0]
    steps_per_batch = seq // OUT_TM
    return pl.pallas_call(
        functools.partial(_outproj_kernel, final),
        grid=(t // OUT_TM,),
        in_specs=[
            pl.BlockSpec((OUT_TM, o_a.shape[1]), lambda i: (i, 0)),
            pl.BlockSpec((OUT_TM, o_b.shape[1]), lambda i: (i, 0)),
            pl.BlockSpec(w_bf16.shape, lambda i: (0, 0)),
            pl.BlockSpec((OUT_TM, d), lambda i: (i, 0)),
            pl.BlockSpec((1, 1, 3 * d), lambda i: (i // steps_per_batch, 0, 0)),
            pl.BlockSpec((1, d), lambda i: (0, 0)),
        ],
        out_specs=pl.BlockSpec((OUT_TM, d), lambda i: (i, 0)),
        out_shape=jax.ShapeDtypeStruct((t, d), F32),
        compiler_params=pltpu.CompilerParams(
            dimension_semantics=("parallel",), vmem_limit_bytes=VMEM_LIMIT),
        name="outproj",
    )(o_a, o_b, w_bf16, x2, mod_l.reshape(bsz, 1, 3 * d), final_g.reshape(1, d))


def kernel(x, c, w_ada, b_ada, norm_g, w_in, lb_logits, hgrn_norm_g, sb_norm_g, w_out, final_g):
    bsz, seq, d = x.shape
    depth = w_ada.shape[0]
    assert d == D_MODEL and w_in.shape[2] == D_IN
    assert seq % max(IN_TM, HG_SB, AT_TQ, OUT_TM) == 0 and AT_TQ == AT_TK
    mod = _adaln_mod(c, w_ada, b_ada)
    x2 = x.reshape(bsz * seq, d)
    for layer in range(depth):
        u32, u16 = _inproj(x2, mod[layer], norm_g[layer], _permute_w_in(w_in[layer]), seq)
        o_a = _hgrn(u32, u16, lb_logits, hgrn_norm_g[layer], layer, bsz, seq)
        o_b = _attn(u32, u16, sb_norm_g[layer], bsz, seq)
        x2 = _outproj(o_a, o_b, w_out[layer].astype(BF16), x2, mod[layer], final_g,
                      layer == depth - 1, seq)
    return x2.reshape(bsz, seq, d)
```

```python
import functools
import math

import jax
import jax.numpy as jnp
import numpy as np
from jax import lax
from jax.experimental import pallas as pl
from jax.experimental.pallas import tpu as pltpu

F32 = jnp.float32
BF16 = jnp.bfloat16
U32 = jnp.uint32

EPS = 1e-6
LOG2E = math.log2(math.e)
LANES = 128
D_MODEL = 1024
HGRN_HEADS = 8
HGRN_DK = 128
SB_HEADS = 16
SB_DH = 64
D_IN = 8 * D_MODEL

VMEM_LIMIT = 56 * 1024 * 1024

IN_TM = 1024
IN_TN = 2048
IN_F32_SECTIONS = (0, 1, 3, 7)
IN_BF16_SECTIONS = (2, 4, 5, 6)
OUT_TM = 512
HG_C = 128
HG_HB = 2
HG_SB = 2048
HG_UNROLL = 2
HG_MXU_LEVELS = (2, 4)
HG_VPU_LEVELS = (8, 16, 32, 64)
HG_SUB = 16
HG_SAFE_LOG2 = 100.0
AT_TQ = 256
AT_TK = 256
AT_HB = 4
AT_DEAD_LOG2 = 160.0


def _dot(a, b):
    return jnp.dot(a, b, preferred_element_type=F32)


def _dot_nt(a, b):
    return lax.dot_general(a, b, (((1,), (1,)), ((), ())), preferred_element_type=F32)


def _dot_tn(a, b):
    return lax.dot_general(a, b, (((0,), (0,)), ((), ())), preferred_element_type=F32)


def _silu(x):
    h = 0.5 * x
    return h + h * jnp.tanh(h)


def _split_bf16(x):
    hi = x.astype(BF16)
    return hi, (x - hi.astype(F32)).astype(BF16)


def _mod_kernel(c_ref, w_ref, b_ref, o_ref):
    ca = _silu(c_ref[...])
    o_ref[0] = jnp.dot(ca, w_ref[0], preferred_element_type=F32,
                       precision=lax.Precision.HIGHEST) + b_ref[0]


def _adaln_mod(c, w_ada, b_ada):
    depth, d, d3 = w_ada.shape
    bsz = c.shape[0]
    tn = d
    return pl.pallas_call(
        _mod_kernel,
        grid=(depth, d3 // tn),
        in_specs=[
            pl.BlockSpec((bsz, d), lambda l, j: (0, 0)),
            pl.BlockSpec((1, d, tn), lambda l, j: (l, 0, j)),
            pl.BlockSpec((1, 1, tn), lambda l, j: (l, 0, j)),
        ],
        out_specs=pl.BlockSpec((1, bsz, tn), lambda l, j: (l, 0, j)),
        out_shape=jax.ShapeDtypeStruct((depth, bsz, d3), F32),
        compiler_params=pltpu.CompilerParams(
            dimension_semantics=("arbitrary", "arbitrary"), vmem_limit_bytes=VMEM_LIMIT),
        name="adaln_mod",
    )(c, w_ada, b_ada.reshape(depth, 1, d3))


def _inproj_kernel(n32, x_ref, mod_ref, g_ref, w_ref, o32_ref, o16_ref, h_ref):
    d = x_ref.shape[1]
    j = pl.program_id(1)

    @pl.when(j == 0)
    def _():
        x = x_ref[...]
        ms = jnp.mean(x * x, axis=-1, keepdims=True)
        y = x * lax.rsqrt(ms + EPS) * g_ref[...]
        shift = mod_ref[0, :, 0:d]
        scale = mod_ref[0, :, d:2 * d]
        h_ref[...] = (y * (1.0 + scale) + shift).astype(BF16)

    r = _dot(h_ref[...], w_ref[...])

    @pl.when(j < n32)
    def _():
        o32_ref[...] = r

    @pl.when(j >= n32)
    def _():
        o16_ref[...] = r.astype(BF16)


def _inproj(x2, mod_l, g, w_perm, seq):
    t, d = x2.shape
    bsz = mod_l.shape[0]
    steps_per_batch = seq // IN_TM
    w32 = len(IN_F32_SECTIONS) * D_MODEL
    w16 = len(IN_BF16_SECTIONS) * D_MODEL
    n32 = w32 // IN_TN
    n16 = w16 // IN_TN
    return pl.pallas_call(
        functools.partial(_inproj_kernel, n32),
        grid=(t // IN_TM, n32 + n16),
        in_specs=[
            pl.BlockSpec((IN_TM, d), lambda i, j: (i, 0)),
            pl.BlockSpec((1, 1, 3 * d), lambda i, j: (i // steps_per_batch, 0, 0)),
            pl.BlockSpec((1, d), lambda i, j: (0, 0)),
            pl.BlockSpec((d, IN_TN), lambda i, j: (0, j)),
        ],
        out_specs=[
            pl.BlockSpec((IN_TM, IN_TN), lambda i, j: (i, jnp.minimum(j, n32 - 1))),
            pl.BlockSpec((IN_TM, IN_TN), lambda i, j: (i, jnp.maximum(j - n32, 0))),
        ],
        out_shape=[jax.ShapeDtypeStruct((t, w32), F32), jax.ShapeDtypeStruct((t, w16), BF16)],
        scratch_shapes=[pltpu.VMEM((IN_TM, d), BF16)],
        compiler_params=pltpu.CompilerParams(
            dimension_semantics=("parallel", "arbitrary"), vmem_limit_bytes=VMEM_LIMIT),
        name="inproj",
    )(x2, mod_l.reshape(bsz, 1, 3 * d), g.reshape(1, d), w_perm)


def _permute_w_in(w):
    sec = [w[:, s * D_MODEL:(s + 1) * D_MODEL] for s in range(8)]
    sec[4] = sec[4] * (SB_DH ** -0.5 * LOG2E)
    cols = [sec[s] for s in IN_F32_SECTIONS + IN_BF16_SECTIONS]
    return jnp.concatenate(cols, axis=1).astype(BF16)


def _hgrn_tables():
    c = HG_C
    t = np.arange(c)[:, None]
    j = np.arange(c)[None, :]
    blocks = [(j <= t)]
    for h in HG_MXU_LEVELS:
        mid = (t // (2 * h)) * 2 * h + h - 1
        blocks.append(((j > mid) & (j <= t)) | ((j > t) & (j <= mid)))
    pmat = np.concatenate(blocks, axis=0).astype(np.float32)
    x = t ^ j
    lv = np.where(j > t, -1, np.where(j == t, 0, 1 + np.floor(np.log2(np.maximum(x, 1))))).astype(np.int32)
    return jnp.asarray(pmat, BF16), jnp.asarray(lv)


def _hgrn_kernel(layer, q_ref, f_ref, i_ref, g_ref, lbl_ref, gn_ref, pmat_ref, lv_ref,
                 o_ref, st_ref, d_ref, dec1_ref, kk_ref, hl_ref):
    c = HG_C
    heads = range(HG_HB)

    @pl.when(pl.program_id(2) == 0)
    def _():
        st_ref[...] = jnp.zeros_like(st_ref)

    if layer > 0:
        lg = lbl_ref[...]
        e = jnp.exp(lg - jnp.max(lg, axis=0, keepdims=True))
        p = e / jnp.sum(e, axis=0, keepdims=True)
        lb = p[1:2, :]
        for l in range(2, layer + 1):
            lb = lb + p[l:l + 1, :]
        log_lb = jnp.log(lb)
        log1m_lb = jnp.log(1.0 - lb)
        one_m_lb = 1.0 - lb

    gn = gn_ref[...]
    row_id = lax.broadcasted_iota(jnp.int32, (c, LANES), 0)
    half_sign = {h: jnp.where((row_id & h) != 0, 1.0, -1.0) for h in HG_VPU_LEVELS}

    w = HG_HB * LANES

    def gates(rows):
        fl = f_ref[rows, :]
        e = jnp.exp2(jnp.abs(fl) * -LOG2E)
        ope = 1.0 + e
        ls = jnp.minimum(fl, 0.0) - jnp.log(ope)
        r = 1.0 / ope
        sig = jnp.where(fl >= 0.0, 1.0, e) * r
        k_all = jnp.where(fl >= 0.0, e, 1.0) * r
        if layer == 0:
            return ls * LOG2E, sig, k_all
        bb = log1m_lb + ls
        lf = jnp.maximum(log_lb, bb) + jnp.log(1.0 + jnp.exp(-jnp.abs(log_lb - bb)))
        return lf * LOG2E, lb + one_m_lb * sig, one_m_lb * k_all

    chunks = range(HG_UNROLL)
    n_groups = HG_SB // (c * HG_UNROLL)
    odd_row_w = (lax.broadcasted_iota(jnp.int32, (c, w), 0) & 1) == 1

    def group_rows(gi):
        return [pl.ds(pl.multiple_of((gi * HG_UNROLL + t) * c, c), c) for t in chunks]

    def front(gi):
        rows = group_rows(gi)
        lfk = [gates(rows[t]) for t in chunks]
        for t in chunks:
            dec1_ref[t] = jnp.where(odd_row_w, lfk[t][1], 1.0).astype(BF16)
            kk_ref[t] = lfk[t][2].astype(BF16)
        splits = [_split_bf16(lfk[t][0]) for t in chunks]
        hl_ref[0] = jnp.concatenate([s[0] for s in splits], axis=1)
        hl_ref[1] = jnp.concatenate([s[1] for s in splits], axis=1)
        cum = pmat_ref[0:c, :]
        d_ref[0:c, :] = _dot(cum, hl_ref[0]) + _dot(cum, hl_ref[1])
        b3 = d_ref[0:c, :].reshape(c // HG_SUB, HG_SUB, HG_UNROLL * w)
        last = b3[:, HG_SUB - 1:HG_SUB, :]
        prev = jnp.concatenate([jnp.zeros_like(last[0:1]), last[:-1]], axis=0)
        return (jnp.max(prev - last) < HG_SAFE_LOG2).astype(jnp.int32)

    def back(gi, direct_sub):
        rows = group_rows(gi)
        lv = lv_ref[...]
        streams = [(t, hd) for t in chunks for hd in heads]
        sl_in = {s: slice(s[1] * LANES, (s[1] + 1) * LANES) for s in streams}
        sl_d = {s: slice(s[0] * w + s[1] * LANES, s[0] * w + (s[1] + 1) * LANES) for s in streams}
        qs = {s: _silu(q_ref[rows[s[0]], sl_in[s]]).astype(BF16) for s in streams}
        kk = {s: kk_ref[s[0], :, sl_in[s]] for s in streams}
        iv = {s: i_ref[rows[s[0]], sl_in[s]] for s in streams}
        bs = {s: d_ref[0:c, sl_d[s]] for s in streams}

        def add_level(idx, dec_of):
            decs = {s: dec_of(s) for s in streams}
            for s in streams:
                s_h = _dot_nt(qs[s] * decs[s], kk[s] * decs[s])
                sc[s] = jnp.where(lv == idx, s_h, sc[s])

        def block_level(h):
            def dec_of(s):
                b3 = bs[s].reshape(c // (2 * h), 2 * h, LANES)
                d = (b3 - b3[:, h - 1:h, :]).reshape(c, LANES) * half_sign[h]
                return jnp.exp2(d).astype(BF16)
            add_level(1 + int(math.log2(h)), dec_of)

        if direct_sub:
            sub_ops = {}
            for s in streams:
                b3 = bs[s].reshape(c // HG_SUB, HG_SUB, LANES)
                last = b3[:, HG_SUB - 1:HG_SUB, :]
                start = jnp.concatenate([jnp.zeros_like(last[0:1]), last[:-1]], axis=0)
                dq = (b3 - start).reshape(c, LANES)
                sub_ops[s] = (qs[s] * jnp.exp2(dq).astype(BF16), kk[s] * jnp.exp2(-dq).astype(BF16))
            in_sub = (lv >= 0) & (lv <= int(math.log2(HG_SUB)))
            sc = {s: jnp.where(in_sub, _dot_nt(*sub_ops[s]), 0.0) for s in streams}
            for h in HG_VPU_LEVELS:
                if h >= HG_SUB:
                    block_level(h)
        else:
            lower = pmat_ref[c:, :]
            d_ref[c:, :] = _dot(lower, hl_ref[0]) + _dot(lower, hl_ref[1])
            sc = {s: jnp.where(lv == 0, _dot_nt(qs[s], kk[s]), 0.0) for s in streams}
            add_level(1, lambda s: dec1_ref[s[0], :, sl_in[s]])
            for n, h in enumerate(HG_MXU_LEVELS):
                add_level(1 + int(math.log2(h)),
                          lambda s, n=n: jnp.exp2(d_ref[(n + 1) * c:(n + 2) * c, sl_d[s]]).astype(BF16))
            for h in HG_VPU_LEVELS:
                block_level(h)

        o_intra = {s: _dot(sc[s].astype(BF16), iv[s]) for s in streams}
        q_dec = {s: qs[s] * jnp.exp2(bs[s]).astype(BF16) for s in streams}
        b_last = {s: bs[s][c - 1:c, :] for s in streams}
        k_dec = {s: kk[s] * jnp.exp2(b_last[s] - bs[s]).astype(BF16) for s in streams}

        o_inter = {}
        for hd in heads:
            st = st_ref[hd]
            for t in chunks:
                s = (t, hd)
                o_inter[s] = _dot_nt(q_dec[s], st.astype(BF16))
                st = st * jnp.exp2(b_last[s]) + _dot_tn(iv[s], k_dec[s])
            st_ref[hd] = st

        next_safe = front(jnp.minimum(gi + 1, n_groups - 1))

        for s in streams:
            o = o_intra[s] + o_inter[s]
            ms = jnp.mean(o * o, axis=-1, keepdims=True)
            y = o * lax.rsqrt(ms + EPS) * gn * _silu(g_ref[rows[s[0]], sl_in[s]])
            o_ref[rows[s[0]], sl_in[s]] = y.astype(o_ref.dtype)
        return next_safe

    def chunk_group(gi, safe):
        return lax.cond(safe == 1, lambda: back(gi, True), lambda: back(gi, False))

    lax.fori_loop(0, n_groups, chunk_group, front(0))


def _hgrn(u32, u16, lb_logits, gn, layer, bsz, seq):
    t = u32.shape[0]
    w = HG_HB * LANES
    nsb = seq // HG_SB
    depth = lb_logits.shape[0]
    pmat, lv = _hgrn_tables()
    hpb = D_MODEL // w

    def col(section):
        return lambda b, h, s: (b * nsb + s, section * hpb + h)

    return pl.pallas_call(
        functools.partial(_hgrn_kernel, layer),
        grid=(bsz, HGRN_HEADS // HG_HB, nsb),
        in_specs=[
            pl.BlockSpec((HG_SB, w), col(0)),
            pl.BlockSpec((HG_SB, w), col(1)),
            pl.BlockSpec((HG_SB, w), col(0)),
            pl.BlockSpec((HG_SB, w), col(2)),
            pl.BlockSpec((depth, w), lambda b, h, s: (0, h)),
            pl.BlockSpec((1, LANES), lambda b, h, s: (0, 0)),
            pl.BlockSpec(pmat.shape, lambda b, h, s: (0, 0)),
            pl.BlockSpec(lv.shape, lambda b, h, s: (0, 0)),
        ],
        out_specs=pl.BlockSpec((HG_SB, w), lambda b, h, s: (b * nsb + s, h)),
        out_shape=jax.ShapeDtypeStruct((t, D_MODEL), BF16),
        scratch_shapes=[
            pltpu.VMEM((HG_HB, LANES, LANES), F32),
            pltpu.VMEM((pmat.shape[0], HG_UNROLL * w), F32),
            pltpu.VMEM((HG_UNROLL, HG_C, w), BF16),
            pltpu.VMEM((HG_UNROLL, HG_C, w), BF16),
            pltpu.VMEM((2, HG_C, HG_UNROLL * w), BF16),
        ],
        compiler_params=pltpu.CompilerParams(
            dimension_semantics=("parallel", "parallel", "arbitrary"),
            vmem_limit_bytes=VMEM_LIMIT),
        name="hgrn2",
    )(u32, u32, u16, u32, lb_logits, gn.reshape(1, LANES), pmat, lv)


def _attn_tables():
    j = np.arange(AT_TK)
    after = (j[:, None] >= j[None, :]).astype(np.float32)
    lane = np.arange(AT_HB * SB_DH)
    seg = (lane[:, None] // SB_DH == lane[None, :] // SB_DH).astype(np.float32) / SB_DH
    return jnp.asarray(after, BF16), jnp.asarray(seg, BF16)


def _attn_kernel(q_ref, k_ref, v_ref, g_ref, gn_ref, u_ref, seg_ref, o_ref,
                 vm_ref, acc_ref, car_ref):
    qi = pl.program_id(2)
    heads = range(AT_HB)
    w = AT_HB * SB_DH
    lane = lax.broadcasted_iota(jnp.int32, (1, w), 1)
    head_lanes = [(lane >= hd * SB_DH) & (lane < (hd + 1) * SB_DH) for hd in heads]

    @pl.when(qi == 0)
    def _():
        v = v_ref[...]
        for hd in heads:
            vm_ref[hd] = jnp.where(head_lanes[hd], v, jnp.zeros_like(v))

    q = q_ref[...]
    qm = [jnp.where(head_lanes[hd], q, jnp.zeros_like(q)) for hd in heads]
    acc_ref[...] = jnp.zeros_like(acc_ref)
    car_ref[...] = jnp.zeros_like(car_ref)
    umat = u_ref[...]
    row = lax.broadcasted_iota(jnp.int32, (AT_TQ, AT_TK), 0)
    colm = lax.broadcasted_iota(jnp.int32, (AT_TQ, AT_TK), 1)
    causal = colm < row

    def tiles(js, masked):
        keys = [pl.ds(pl.multiple_of(j * AT_TK, AT_TK), AT_TK) for j in js]
        streams = [(t, hd) for t in range(len(js)) for hd in heads]
        kb = [k_ref[ks, :] for ks in keys]
        zs = {s: _dot_nt(qm[s[1]], kb[s[0]]) for s in streams}
        parts = {}
        for s in streams:
            z = zs[s]
            sp = jnp.maximum(z, 0.0) + jnp.log(1.0 + jnp.exp2(-jnp.abs(z))) * LOG2E
            if masked[s[0]]:
                sp = jnp.where(causal, sp, 0.0)
            parts[s] = _split_bf16(sp)
        rs = {s: _dot(parts[s][0], umat) + _dot(parts[s][1], umat) for s in streams}
        pv = None
        for hd in heads:
            car = car_ref[hd]
            for t in range(len(js)):
                s = (t, hd)
                wgt = jnp.exp2(zs[s] - car - rs[s])
                if masked[t]:
                    wgt = jnp.where(causal, wgt, 0.0)
                d = _dot(wgt.astype(BF16), vm_ref[hd, keys[t], :])
                pv = d if pv is None else pv + d
                car = car + rs[s][:, 0:1]
            car_ref[hd] = car
        acc_ref[...] += pv

    def all_underflowed():
        m = car_ref[0]
        for hd in range(1, AT_HB):
            m = jnp.minimum(m, car_ref[hd])
        return (jnp.min(m) >= AT_DEAD_LOG2).astype(jnp.int32)

    @pl.when(qi == 0)
    def _():
        tiles([qi], [True])

    @pl.when(qi > 0)
    def _():
        tiles([qi, qi - 1], [True, False])

    rem = jnp.maximum(qi - 1, 0)
    n_pairs = lax.shift_right_logical(rem, 1)

    def cond(state):
        it, dead = state
        return (it < n_pairs) & (dead == 0)

    def body(state):
        it, _ = state
        j = qi - 2 - 2 * it
        tiles([j, j - 1], [False, False])
        return it + 1, all_underflowed()

    _, dead = lax.while_loop(cond, body, (jnp.int32(0), all_underflowed()))

    @pl.when(((rem & 1) == 1) & (dead == 0))
    def _():
        tiles([0], [False])

    o = acc_ref[...]
    hi, lo = _split_bf16(o * o)
    seg = seg_ref[...]
    ms = _dot(hi, seg) + _dot(lo, seg)
    y = o * lax.rsqrt(ms + EPS) * gn_ref[...] * _silu(g_ref[...])
    o_ref[...] = y.astype(o_ref.dtype)


def _attn(u32, u16, gn, bsz, seq):
    t = u32.shape[0]
    w = AT_HB * SB_DH
    nq = seq // AT_TQ
    umat, seg = _attn_tables()
    hpb = D_MODEL // w
    gn2 = jnp.tile(gn.reshape(1, SB_DH), (1, AT_HB))

    return pl.pallas_call(
        _attn_kernel,
        grid=(bsz, SB_HEADS // AT_HB, nq),
        in_specs=[
            pl.BlockSpec((AT_TQ, w), lambda b, h, i: (b * nq + i, 1 * hpb + h)),
            pl.BlockSpec((seq, w), lambda b, h, i: (b, 2 * hpb + h)),
            pl.BlockSpec((seq, w), lambda b, h, i: (b, 3 * hpb + h)),
            pl.BlockSpec((AT_TQ, w), lambda b, h, i: (b * nq + i, 3 * hpb + h)),
            pl.BlockSpec((1, w), lambda b, h, i: (0, 0)),
            pl.BlockSpec(umat.shape, lambda b, h, i: (0, 0)),
            pl.BlockSpec(seg.shape, lambda b, h, i: (0, 0)),
        ],
        out_specs=pl.BlockSpec((AT_TQ, w), lambda b, h, i: (b * nq + i, h)),
        out_shape=jax.ShapeDtypeStruct((t, D_MODEL), BF16),
        scratch_shapes=[
            pltpu.VMEM((AT_HB, seq, w), BF16),
            pltpu.VMEM((AT_TQ, w), F32),
            pltpu.VMEM((AT_HB, AT_TQ, 1), F32),
        ],
        compiler_params=pltpu.CompilerParams(
            dimension_semantics=("parallel", "parallel", "arbitrary"),
            vmem_limit_bytes=VMEM_LIMIT),
        name="stickbreak",
    )(u16, u16, u16, u32, gn2, umat, seg)


def _outproj_kernel(final, oa_ref, ob_ref, w_ref, x_ref, mod_ref, fg_ref, o_ref):
    d = x_ref.shape[1]
    da = oa_ref.shape[1]
    y = _dot(oa_ref[...], w_ref[0:da, :]) + _dot(ob_ref[...], w_ref[da:, :])
    gate = mod_ref[0, :, 2 * d:3 * d]
    xn = x_ref[...] + gate * y
    if final:
        ms = jnp.mean(xn * xn, axis=-1, keepdims=True)
        xn = xn * lax.rsqrt(ms + EPS) * fg_ref[...]
    o_ref[...] = xn


def _outproj(o_a, o_b, w_bf16, x2, mod_l, final_g, final, seq):
    t, d = x2.shape
    bsz = mod_l.shape[0]
    steps_per_batch = seq // OUT_TM
    return pl.pallas_call(
        functools.partial(_outproj_kernel, final),
        grid=(t // OUT_TM,),
        in_specs=[
            pl.BlockSpec((OUT_TM, o_a.shape[1]), lambda i: (i, 0)),
            pl.BlockSpec((OUT_TM, o_b.shape[1]), lambda i: (i, 0)),
            pl.BlockSpec(w_bf16.shape, lambda i: (0, 0)),
            pl.BlockSpec((OUT_TM, d), lambda i: (i, 0)),
            pl.BlockSpec((1, 1, 3 * d), lambda i: (i // steps_per_batch, 0, 0)),
            pl.BlockSpec((1, d), lambda i: (0, 0)),
        ],
        out_specs=pl.BlockSpec((OUT_TM, d), lambda i: (i, 0)),
        out_shape=jax.ShapeDtypeStruct((t, d), F32),
        compiler_params=pltpu.CompilerParams(
            dimension_semantics=("parallel",), vmem_limit_bytes=VMEM_LIMIT),
        name="outproj",
    )(o_a, o_b, w_bf16, x2, mod_l.reshape(bsz, 1, 3 * d), final_g.reshape(1, d))


def kernel(x, c, w_ada, b_ada, norm_g, w_in, lb_logits, hgrn_norm_g, sb_norm_g, w_out, final_g):
    bsz, seq, d = x.shape
    depth = w_ada.shape[0]
    assert d == D_MODEL and w_in.shape[2] == D_IN
    assert seq % max(IN_TM, HG_SB, AT_TQ, OUT_TM) == 0 and AT_TQ == AT_TK
    mod = _adaln_mod(c, w_ada, b_ada)
    x2 = x.reshape(bsz * seq, d)
    for layer in range(depth):
        u32, u16 = _inproj(x2, mod[layer], norm_g[layer], _permute_w_in(w_in[layer]), seq)
        o_a = _hgrn(u32, u16, lb_logits, hgrn_norm_g[layer], layer, bsz, seq)
        o_b = _attn(u32, u16, sb_norm_g[layer], bsz, seq)
        x2 = _outproj(o_a, o_b, w_out[layer].astype(BF16), x2, mod[layer], final_g,
                      layer == depth - 1, seq)
    return x2.reshape(bsz, seq, d)
```

```python
import functools
import math

import jax
import jax.numpy as jnp
import numpy as np
from jax import lax
from jax.experimental import pallas as pl
from jax.experimental.pallas import tpu as pltpu

F32 = jnp.float32
BF16 = jnp.bfloat16
U32 = jnp.uint32

EPS = 1e-6
LOG2E = math.log2(math.e)
LANES = 128
D_MODEL = 1024
HGRN_HEADS = 8
HGRN_DK = 128
SB_HEADS = 16
SB_DH = 64
D_IN = 8 * D_MODEL

VMEM_LIMIT = 56 * 1024 * 1024

IN_TM = 1024
IN_TN = 2048
IN_F32_SECTIONS = (0, 1, 3, 7)
IN_BF16_SECTIONS = (2, 4, 5, 6)
OUT_TM = 512
HG_C = 128
HG_HB = 2
HG_SB = 2048
HG_UNROLL = 2
HG_MXU_LEVELS = (2, 4)
HG_VPU_LEVELS = (8, 16, 32, 64)
HG_SUB = 16
HG_SAFE_LOG2 = 100.0
AT_TQ = 256
AT_TK = 256
AT_HB = 4
AT_QB = 2
AT_DEAD_LOG2 = 160.0


def _dot(a, b):
    return jnp.dot(a, b, preferred_element_type=F32)


def _dot_nt(a, b):
    return lax.dot_general(a, b, (((1,), (1,)), ((), ())), preferred_element_type=F32)


def _dot_tn(a, b):
    return lax.dot_general(a, b, (((0,), (0,)), ((), ())), preferred_element_type=F32)


def _silu(x):
    h = 0.5 * x
    return h + h * jnp.tanh(h)


def _split_bf16(x):
    hi = x.astype(BF16)
    return hi, (x - hi.astype(F32)).astype(BF16)


def _mod_kernel(c_ref, w_ref, b_ref, o_ref):
    ca = _silu(c_ref[...])
    o_ref[0] = jnp.dot(ca, w_ref[0], preferred_element_type=F32,
                       precision=lax.Precision.HIGHEST) + b_ref[0]


def _adaln_mod(c, w_ada, b_ada):
    depth, d, d3 = w_ada.shape
    bsz = c.shape[0]
    tn = d
    return pl.pallas_call(
        _mod_kernel,
        grid=(depth, d3 // tn),
        in_specs=[
            pl.BlockSpec((bsz, d), lambda l, j: (0, 0)),
            pl.BlockSpec((1, d, tn), lambda l, j: (l, 0, j)),
            pl.BlockSpec((1, 1, tn), lambda l, j: (l, 0, j)),
        ],
        out_specs=pl.BlockSpec((1, bsz, tn), lambda l, j: (l, 0, j)),
        out_shape=jax.ShapeDtypeStruct((depth, bsz, d3), F32),
        compiler_params=pltpu.CompilerParams(
            dimension_semantics=("arbitrary", "arbitrary"), vmem_limit_bytes=VMEM_LIMIT),
        name="adaln_mod",
    )(c, w_ada, b_ada.reshape(depth, 1, d3))


def _inproj_kernel(n32, x_ref, mod_ref, g_ref, w_ref, o32_ref, o16_ref, h_ref):
    d = x_ref.shape[1]
    j = pl.program_id(1)

    @pl.when(j == 0)
    def _():
        x = x_ref[...]
        ms = jnp.mean(x * x, axis=-1, keepdims=True)
        y = x * lax.rsqrt(ms + EPS) * g_ref[...]
        shift = mod_ref[0, :, 0:d]
        scale = mod_ref[0, :, d:2 * d]
        h_ref[...] = (y * (1.0 + scale) + shift).astype(BF16)

    r = _dot(h_ref[...], w_ref[...])

    @pl.when(j < n32)
    def _():
        o32_ref[...] = r

    @pl.when(j >= n32)
    def _():
        o16_ref[...] = r.astype(BF16)


def _inproj(x2, mod_l, g, w_perm, seq):
    t, d = x2.shape
    bsz = mod_l.shape[0]
    steps_per_batch = seq // IN_TM
    w32 = len(IN_F32_SECTIONS) * D_MODEL
    w16 = len(IN_BF16_SECTIONS) * D_MODEL
    n32 = w32 // IN_TN
    n16 = w16 // IN_TN
    return pl.pallas_call(
        functools.partial(_inproj_kernel, n32),
        grid=(t // IN_TM, n32 + n16),
        in_specs=[
            pl.BlockSpec((IN_TM, d), lambda i, j: (i, 0)),
            pl.BlockSpec((1, 1, 3 * d), lambda i, j: (i // steps_per_batch, 0, 0)),
            pl.BlockSpec((1, d), lambda i, j: (0, 0)),
            pl.BlockSpec((d, IN_TN), lambda i, j: (0, j)),
        ],
        out_specs=[
            pl.BlockSpec((IN_TM, IN_TN), lambda i, j: (i, jnp.minimum(j, n32 - 1))),
            pl.BlockSpec((IN_TM, IN_TN), lambda i, j: (i, jnp.maximum(j - n32, 0))),
        ],
        out_shape=[jax.ShapeDtypeStruct((t, w32), F32), jax.ShapeDtypeStruct((t, w16), BF16)],
        scratch_shapes=[pltpu.VMEM((IN_TM, d), BF16)],
        compiler_params=pltpu.CompilerParams(
            dimension_semantics=("parallel", "arbitrary"), vmem_limit_bytes=VMEM_LIMIT),
        name="inproj",
    )(x2, mod_l.reshape(bsz, 1, 3 * d), g.reshape(1, d), w_perm)


def _permute_w_in(w):
    sec = [w[:, s * D_MODEL:(s + 1) * D_MODEL] for s in range(8)]
    sec[4] = sec[4] * (SB_DH ** -0.5 * LOG2E)
    cols = [sec[s] for s in IN_F32_SECTIONS + IN_BF16_SECTIONS]
    return jnp.concatenate(cols, axis=1).astype(BF16)


def _hgrn_tables():
    c = HG_C
    t = np.arange(c)[:, None]
    j = np.arange(c)[None, :]
    blocks = [(j <= t)]
    for h in HG_MXU_LEVELS:
        mid = (t // (2 * h)) * 2 * h + h - 1
        blocks.append(((j > mid) & (j <= t)) | ((j > t) & (j <= mid)))
    pmat = np.concatenate(blocks, axis=0).astype(np.float32)
    x = t ^ j
    lv = np.where(j > t, -1, np.where(j == t, 0, 1 + np.floor(np.log2(np.maximum(x, 1))))).astype(np.int32)
    return jnp.asarray(pmat, BF16), jnp.asarray(lv)


def _hgrn_kernel(layer, q_ref, f_ref, i_ref, g_ref, lbl_ref, gn_ref, pmat_ref, lv_ref,
                 o_ref, st_ref, d_ref, dec1_ref, kk_ref, hl_ref):
    c = HG_C
    heads = range(HG_HB)

    @pl.when(pl.program_id(2) == 0)
    def _():
        st_ref[...] = jnp.zeros_like(st_ref)

    if layer > 0:
        lg = lbl_ref[...]
        e = jnp.exp(lg - jnp.max(lg, axis=0, keepdims=True))
        p = e / jnp.sum(e, axis=0, keepdims=True)
        lb = p[1:2, :]
        for l in range(2, layer + 1):
            lb = lb + p[l:l + 1, :]
        log_lb = jnp.log(lb)
        log1m_lb = jnp.log(1.0 - lb)
        one_m_lb = 1.0 - lb

    gn = gn_ref[...]
    row_id = lax.broadcasted_iota(jnp.int32, (c, LANES), 0)
    half_sign = {h: jnp.where((row_id & h) != 0, 1.0, -1.0) for h in HG_VPU_LEVELS}

    w = HG_HB * LANES

    def gates(rows):
        fl = f_ref[rows, :]
        e = jnp.exp2(jnp.abs(fl) * -LOG2E)
        ope = 1.0 + e
        ls = jnp.minimum(fl, 0.0) - jnp.log(ope)
        r = 1.0 / ope
        sig = jnp.where(fl >= 0.0, 1.0, e) * r
        k_all = jnp.where(fl >= 0.0, e, 1.0) * r
        if layer == 0:
            return ls * LOG2E, sig, k_all
        bb = log1m_lb + ls
        lf = jnp.maximum(log_lb, bb) + jnp.log(1.0 + jnp.exp(-jnp.abs(log_lb - bb)))
        return lf * LOG2E, lb + one_m_lb * sig, one_m_lb * k_all

    chunks = range(HG_UNROLL)
    n_groups = HG_SB // (c * HG_UNROLL)
    odd_row_w = (lax.broadcasted_iota(jnp.int32, (c, w), 0) & 1) == 1

    def group_rows(gi):
        return [pl.ds(pl.multiple_of((gi * HG_UNROLL + t) * c, c), c) for t in chunks]

    def front(gi):
        rows = group_rows(gi)
        lfk = [gates(rows[t]) for t in chunks]
        for t in chunks:
            dec1_ref[t] = jnp.where(odd_row_w, lfk[t][1], 1.0).astype(BF16)
            kk_ref[t] = lfk[t][2].astype(BF16)
        splits = [_split_bf16(lfk[t][0]) for t in chunks]
        hl_ref[0] = jnp.concatenate([s[0] for s in splits], axis=1)
        hl_ref[1] = jnp.concatenate([s[1] for s in splits], axis=1)
        cum = pmat_ref[0:c, :]
        d_ref[0:c, :] = _dot(cum, hl_ref[0]) + _dot(cum, hl_ref[1])
        b3 = d_ref[0:c, :].reshape(c // HG_SUB, HG_SUB, HG_UNROLL * w)
        last = b3[:, HG_SUB - 1:HG_SUB, :]
        prev = jnp.concatenate([jnp.zeros_like(last[0:1]), last[:-1]], axis=0)
        return (jnp.max(prev - last) < HG_SAFE_LOG2).astype(jnp.int32)

    def back(gi, direct_sub):
        rows = group_rows(gi)
        lv = lv_ref[...]
        streams = [(t, hd) for t in chunks for hd in heads]
        sl_in = {s: slice(s[1] * LANES, (s[1] + 1) * LANES) for s in streams}
        sl_d = {s: slice(s[0] * w + s[1] * LANES, s[0] * w + (s[1] + 1) * LANES) for s in streams}
        qs = {s: _silu(q_ref[rows[s[0]], sl_in[s]]).astype(BF16) for s in streams}
        kk = {s: kk_ref[s[0], :, sl_in[s]] for s in streams}
        iv = {s: i_ref[rows[s[0]], sl_in[s]] for s in streams}
        bs = {s: d_ref[0:c, sl_d[s]] for s in streams}

        def add_level(idx, dec_of):
            decs = {s: dec_of(s) for s in streams}
            for s in streams:
                s_h = _dot_nt(qs[s] * decs[s], kk[s] * decs[s])
                sc[s] = jnp.where(lv == idx, s_h, sc[s])

        def block_level(h):
            def dec_of(s):
                b3 = bs[s].reshape(c // (2 * h), 2 * h, LANES)
                d = (b3 - b3[:, h - 1:h, :]).reshape(c, LANES) * half_sign[h]
                return jnp.exp2(d).astype(BF16)
            add_level(1 + int(math.log2(h)), dec_of)

        if direct_sub:
            sub_ops = {}
            for s in streams:
                b3 = bs[s].reshape(c // HG_SUB, HG_SUB, LANES)
                last = b3[:, HG_SUB - 1:HG_SUB, :]
                start = jnp.concatenate([jnp.zeros_like(last[0:1]), last[:-1]], axis=0)
                dq = (b3 - start).reshape(c, LANES)
                sub_ops[s] = (qs[s] * jnp.exp2(dq).astype(BF16), kk[s] * jnp.exp2(-dq).astype(BF16))
            in_sub = (lv >= 0) & (lv <= int(math.log2(HG_SUB)))
            sc = {s: jnp.where(in_sub, _dot_nt(*sub_ops[s]), 0.0) for s in streams}
            for h in HG_VPU_LEVELS:
                if h >= HG_SUB:
                    block_level(h)
        else:
            lower = pmat_ref[c:, :]
            d_ref[c:, :] = _dot(lower, hl_ref[0]) + _dot(lower, hl_ref[1])
            sc = {s: jnp.where(lv == 0, _dot_nt(qs[s], kk[s]), 0.0) for s in streams}
            add_level(1, lambda s: dec1_ref[s[0], :, sl_in[s]])
            for n, h in enumerate(HG_MXU_LEVELS):
                add_level(1 + int(math.log2(h)),
                          lambda s, n=n: jnp.exp2(d_ref[(n + 1) * c:(n + 2) * c, sl_d[s]]).astype(BF16))
            for h in HG_VPU_LEVELS:
                block_level(h)

        o_intra = {s: _dot(sc[s].astype(BF16), iv[s]) for s in streams}
        q_dec = {s: qs[s] * jnp.exp2(bs[s]).astype(BF16) for s in streams}
        b_last = {s: bs[s][c - 1:c, :] for s in streams}
        k_dec = {s: kk[s] * jnp.exp2(b_last[s] - bs[s]).astype(BF16) for s in streams}

        o_inter = {}
        for hd in heads:
            st = st_ref[hd]
            for t in chunks:
                s = (t, hd)
                o_inter[s] = _dot_nt(q_dec[s], st.astype(BF16))
                st = st * jnp.exp2(b_last[s]) + _dot_tn(iv[s], k_dec[s])
            st_ref[hd] = st

        next_safe = front(jnp.minimum(gi + 1, n_groups - 1))

        for s in streams:
            o = o_intra[s] + o_inter[s]
            ms = jnp.mean(o * o, axis=-1, keepdims=True)
            y = o * lax.rsqrt(ms + EPS) * gn * _silu(g_ref[rows[s[0]], sl_in[s]])
            o_ref[rows[s[0]], sl_in[s]] = y.astype(o_ref.dtype)
        return next_safe

    def chunk_group(gi, safe):
        return lax.cond(safe == 1, lambda: back(gi, True), lambda: back(gi, False))

    lax.fori_loop(0, n_groups, chunk_group, front(0))


def _hgrn(u32, u16, lb_logits, gn, layer, bsz, seq):
    t = u32.shape[0]
    w = HG_HB * LANES
    nsb = seq // HG_SB
    depth = lb_logits.shape[0]
    pmat, lv = _hgrn_tables()
    hpb = D_MODEL // w

    def col(section):
        return lambda b, h, s: (b * nsb + s, section * hpb + h)

    return pl.pallas_call(
        functools.partial(_hgrn_kernel, layer),
        grid=(bsz, HGRN_HEADS // HG_HB, nsb),
        in_specs=[
            pl.BlockSpec((HG_SB, w), col(0)),
            pl.BlockSpec((HG_SB, w), col(1)),
            pl.BlockSpec((HG_SB, w), col(0)),
            pl.BlockSpec((HG_SB, w), col(2)),
            pl.BlockSpec((depth, w), lambda b, h, s: (0, h)),
            pl.BlockSpec((1, LANES), lambda b, h, s: (0, 0)),
            pl.BlockSpec(pmat.shape, lambda b, h, s: (0, 0)),
            pl.BlockSpec(lv.shape, lambda b, h, s: (0, 0)),
        ],
        out_specs=pl.BlockSpec((HG_SB, w), lambda b, h, s: (b * nsb + s, h)),
        out_shape=jax.ShapeDtypeStruct((t, D_MODEL), BF16),
        scratch_shapes=[
            pltpu.VMEM((HG_HB, LANES, LANES), F32),
            pltpu.VMEM((pmat.shape[0], HG_UNROLL * w), F32),
            pltpu.VMEM((HG_UNROLL, HG_C, w), BF16),
            pltpu.VMEM((HG_UNROLL, HG_C, w), BF16),
            pltpu.VMEM((2, HG_C, HG_UNROLL * w), BF16),
        ],
        compiler_params=pltpu.CompilerParams(
            dimension_semantics=("parallel", "parallel", "arbitrary"),
            vmem_limit_bytes=VMEM_LIMIT),
        name="hgrn2",
    )(u32, u32, u16, u32, lb_logits, gn.reshape(1, LANES), pmat, lv)


def _attn_tables():
    j = np.arange(AT_TK)
    after = (j[:, None] >= j[None, :]).astype(np.float32)
    lane = np.arange(AT_HB * SB_DH)
    seg = (lane[:, None] // SB_DH == lane[None, :] // SB_DH).astype(np.float32) / SB_DH
    return jnp.asarray(after, BF16), jnp.asarray(seg, BF16)


def _attn_kernel(q_ref, k_ref, v_ref, g_ref, gn_ref, u_ref, seg_ref, o_ref,
                 vm_ref, acc_ref, car_ref, dead_ref):
    step = pl.program_id(2)
    heads = range(AT_HB)
    w = AT_HB * SB_DH
    lane = lax.broadcasted_iota(jnp.int32, (1, w), 1)
    head_lanes = [(lane >= hd * SB_DH) & (lane < (hd + 1) * SB_DH) for hd in heads]

    @pl.when(step == 0)
    def _():
        v = v_ref[...]
        for hd in heads:
            vm_ref[hd] = jnp.where(head_lanes[hd], v, jnp.zeros_like(v))

    umat = u_ref[...]
    row = lax.broadcasted_iota(jnp.int32, (AT_TQ, AT_TK), 0)
    colm = lax.broadcasted_iota(jnp.int32, (AT_TQ, AT_TK), 1)
    causal = colm < row

    def q_block(sub, carry):
        qi = step * AT_QB + sub
        qrows = pl.ds(pl.multiple_of(sub * AT_TQ, AT_TQ), AT_TQ)
        q = q_ref[qrows, :]
        qm = [jnp.where(head_lanes[hd], q, jnp.zeros_like(q)) for hd in heads]
        acc_ref[...] = jnp.zeros_like(acc_ref)
        car_ref[...] = jnp.zeros_like(car_ref)

        def tiles(js, masked):
            n = len(js)
            keys = [pl.ds(pl.multiple_of(j * AT_TK, AT_TK), AT_TK) for j in js]
            streams = [(t, hd) for t in range(n) for hd in heads]
            kb = [k_ref[ks, :] for ks in keys]
            zs = {s: _dot_nt(qm[s[1]], kb[s[0]]) for s in streams}
            parts = {}
            for s in streams:
                z = zs[s]
                sp = jnp.maximum(z, 0.0) + jnp.log(1.0 + jnp.exp2(-jnp.abs(z))) * LOG2E
                if masked[s[0]]:
                    sp = jnp.where(causal, sp, 0.0)
                parts[s] = _split_bf16(sp)
            rs = {s: _dot(parts[s][0], umat) + _dot(parts[s][1], umat) for s in streams}
            cars = {}
            low = None
            for hd in heads:
                car = car_ref[hd]
                for t in range(n):
                    cars[(t, hd)] = car
                    car = car + rs[(t, hd)][:, 0:1]
                car_ref[hd] = car
                low = car if low is None else jnp.minimum(low, car)
            dead_ref[0] = (jnp.min(low) >= AT_DEAD_LOG2).astype(jnp.int32)
            pv = None
            for s in streams:
                wgt = jnp.exp2(zs[s] - cars[s] - rs[s])
                if masked[s[0]]:
                    wgt = jnp.where(causal, wgt, 0.0)
                d = _dot(wgt.astype(BF16), vm_ref[s[1], keys[s[0]], :])
                pv = d if pv is None else pv + d
            acc_ref[...] += pv

        @pl.when(qi == 0)
        def _():
            tiles([qi], [True])

        @pl.when(qi > 0)
        def _():
            tiles([qi, qi - 1], [True, False])

        rem = jnp.maximum(qi - 1, 0)
        n_pairs = lax.shift_right_logical(rem, 1)

        def cond(state):
            it, dead = state
            return (it < n_pairs) & (dead == 0)

        def body(state):
            it, _ = state
            j = qi - 2 - 2 * it
            tiles([j, j - 1], [False, False])
            return it + 1, dead_ref[0]

        _, dead = lax.while_loop(cond, body, (jnp.int32(0), dead_ref[0]))

        @pl.when(((rem & 1) == 1) & (dead == 0))
        def _():
            tiles([0], [False])

        o = acc_ref[...]
        hi, lo = _split_bf16(o * o)
        seg = seg_ref[...]
        ms = _dot(hi, seg) + _dot(lo, seg)
        y = o * lax.rsqrt(ms + EPS) * gn_ref[...] * _silu(g_ref[qrows, :])
        o_ref[qrows, :] = y.astype(o_ref.dtype)
        return carry

    lax.fori_loop(0, AT_QB, q_block, 0)


def _attn(u32, u16, gn, bsz, seq):
    t = u32.shape[0]
    w = AT_HB * SB_DH
    rows = AT_QB * AT_TQ
    nsteps = seq // rows
    umat, seg = _attn_tables()
    hpb = D_MODEL // w
    gn2 = jnp.tile(gn.reshape(1, SB_DH), (1, AT_HB))

    return pl.pallas_call(
        _attn_kernel,
        grid=(bsz, SB_HEADS // AT_HB, nsteps),
        in_specs=[
            pl.BlockSpec((rows, w), lambda b, h, i: (b * nsteps + i, 1 * hpb + h)),
            pl.BlockSpec((seq, w), lambda b, h, i: (b, 2 * hpb + h)),
            pl.BlockSpec((seq, w), lambda b, h, i: (b, 3 * hpb + h)),
            pl.BlockSpec((rows, w), lambda b, h, i: (b * nsteps + i, 3 * hpb + h)),
            pl.BlockSpec((1, w), lambda b, h, i: (0, 0)),
            pl.BlockSpec(umat.shape, lambda b, h, i: (0, 0)),
            pl.BlockSpec(seg.shape, lambda b, h, i: (0, 0)),
        ],
        out_specs=pl.BlockSpec((rows, w), lambda b, h, i: (b * nsteps + i, h)),
        out_shape=jax.ShapeDtypeStruct((t, D_MODEL), BF16),
        scratch_shapes=[
            pltpu.VMEM((AT_HB, seq, w), BF16),
            pltpu.VMEM((AT_TQ, w), F32),
            pltpu.VMEM((AT_HB, AT_TQ, 1), F32),
            pltpu.SMEM((1,), jnp.int32),
        ],
        compiler_params=pltpu.CompilerParams(
            dimension_semantics=("parallel", "parallel", "arbitrary"),
            vmem_limit_bytes=VMEM_LIMIT),
        name="stickbreak",
    )(u16, u16, u16, u32, gn2, umat, seg)


def _outproj_kernel(final, oa_ref, ob_ref, w_ref, x_ref, mod_ref, fg_ref, o_ref):
    d = x_ref.shape[1]
    da = oa_ref.shape[1]
    y = _dot(oa_ref[...], w_ref[0:da, :]) + _dot(ob_ref[...], w_ref[da:, :])
    gate = mod_ref[0, :, 2 * d:3 * d]
    xn = x_ref[...] + gate * y
    if final:
        ms = jnp.mean(xn * xn, axis=-1, keepdims=True)
        xn = xn * lax.rsqrt(ms + EPS) * fg_ref[...]
    o_ref[...] = xn


def _outproj(o_a, o_b, w_bf16, x2, mod_l, final_g, final, seq):
    t, d = x2.shape
    bsz = mod_l.shape[0]
    steps_per_batch = seq // OUT_TM
    return pl.pallas_call(
        functools.partial(_outproj_kernel, final),
        grid=(t // OUT_TM,),
        in_specs=[
            pl.BlockSpec((OUT_TM, o_a.shape[1]), lambda i: (i, 0)),
            pl.BlockSpec((OUT_TM, o_b.shape[1]), lambda i: (i, 0)),
            pl.BlockSpec(w_bf16.shape, lambda i: (0, 0)),
            pl.BlockSpec((OUT_TM, d), lambda i: (i, 0)),
            pl.BlockSpec((1, 1, 3 * d), lambda i: (i // steps_per_batch, 0, 0)),
            pl.BlockSpec((1, d), lambda i: (0, 0)),
        ],
        out_specs=pl.BlockSpec((OUT_TM, d), lambda i: (i, 0)),
        out_shape=jax.ShapeDtypeStruct((t, d), F32),
        compiler_params=pltpu.CompilerParams(
            dimension_semantics=("parallel",), vmem_limit_bytes=VMEM_LIMIT),
        name="outproj",
    )(o_a, o_b, w_bf16, x2, mod_l.reshape(bsz, 1, 3 * d), final_g.reshape(1, d))


def kernel(x, c, w_ada, b_ada, norm_g, w_in, lb_logits, hgrn_norm_g, sb_norm_g, w_out, final_g):
    bsz, seq, d = x.shape
    depth = w_ada.shape[0]
    assert d == D_MODEL and w_in.shape[2] == D_IN
    assert seq % max(IN_TM, HG_SB, AT_QB * AT_TQ, OUT_TM) == 0 and AT_TQ == AT_TK
    mod = _adaln_mod(c, w_ada, b_ada)
    x2 = x.reshape(bsz * seq, d)
    for layer in range(depth):
        u32, u16 = _inproj(x2, mod[layer], norm_g[layer], _permute_w_in(w_in[layer]), seq)
        o_a = _hgrn(u32, u16, lb_logits, hgrn_norm_g[layer], layer, bsz, seq)
        o_b = _attn(u32, u16, sb_norm_g[layer], bsz, seq)
        x2 = _outproj(o_a, o_b, w_out[layer].astype(BF16), x2, mod[layer], final_g,
                      layer == depth - 1, seq)
    return x2.reshape(bsz, seq, d)
```

```python
import functools
import math

import jax
import jax.numpy as jnp
import numpy as np
from jax import lax
from jax.experimental import pallas as pl
from jax.experimental.pallas import tpu as pltpu

F32 = jnp.float32
BF16 = jnp.bfloat16
U32 = jnp.uint32

EPS = 1e-6
LOG2E = math.log2(math.e)
LANES = 128
D_MODEL = 1024
HGRN_HEADS = 8
HGRN_DK = 128
SB_HEADS = 16
SB_DH = 64
D_IN = 8 * D_MODEL

VMEM_LIMIT = 56 * 1024 * 1024

IN_TM = 1024
IN_TN = 2048
IN_F32_SECTIONS = (0, 1, 3, 7)
IN_BF16_SECTIONS = (2, 4, 5, 6)
OUT_TM = 512
HG_C = 128
HG_HB = 2
HG_SB = 4096
HG_UNROLL = 2
HG_MXU_LEVELS = (2, 4)
HG_VPU_LEVELS = (8, 16, 32, 64)
HG_SUB = 16
HG_SAFE_LOG2 = 100.0
AT_TQ = 256
AT_TK = 256
AT_HB = 4
AT_QB = 4
AT_DEAD_LOG2 = 160.0


def _dot(a, b):
    return jnp.dot(a, b, preferred_element_type=F32)


def _dot_nt(a, b):
    return lax.dot_general(a, b, (((1,), (1,)), ((), ())), preferred_element_type=F32)


def _dot_tn(a, b):
    return lax.dot_general(a, b, (((0,), (0,)), ((), ())), preferred_element_type=F32)


def _silu(x):
    h = 0.5 * x
    return h + h * jnp.tanh(h)


def _split_bf16(x):
    hi = x.astype(BF16)
    return hi, (x - hi.astype(F32)).astype(BF16)


def _mod_kernel(c_ref, w_ref, b_ref, o_ref):
    ca = _silu(c_ref[...])
    o_ref[0] = jnp.dot(ca, w_ref[0], preferred_element_type=F32,
                       precision=lax.Precision.HIGHEST) + b_ref[0]


def _adaln_mod(c, w_ada, b_ada):
    depth, d, d3 = w_ada.shape
    bsz = c.shape[0]
    tn = d
    return pl.pallas_call(
        _mod_kernel,
        grid=(depth, d3 // tn),
        in_specs=[
            pl.BlockSpec((bsz, d), lambda l, j: (0, 0)),
            pl.BlockSpec((1, d, tn), lambda l, j: (l, 0, j)),
            pl.BlockSpec((1, 1, tn), lambda l, j: (l, 0, j)),
        ],
        out_specs=pl.BlockSpec((1, bsz, tn), lambda l, j: (l, 0, j)),
        out_shape=jax.ShapeDtypeStruct((depth, bsz, d3), F32),
        compiler_params=pltpu.CompilerParams(
            dimension_semantics=("arbitrary", "arbitrary"), vmem_limit_bytes=VMEM_LIMIT),
        name="adaln_mod",
    )(c, w_ada, b_ada.reshape(depth, 1, d3))


def _inproj_kernel(n32, x_ref, mod_ref, g_ref, w_ref, o32_ref, o16_ref, h_ref):
    d = x_ref.shape[1]
    j = pl.program_id(1)

    @pl.when(j == 0)
    def _():
        x = x_ref[...]
        ms = jnp.mean(x * x, axis=-1, keepdims=True)
        y = x * lax.rsqrt(ms + EPS) * g_ref[...]
        shift = mod_ref[0, :, 0:d]
        scale = mod_ref[0, :, d:2 * d]
        h_ref[...] = (y * (1.0 + scale) + shift).astype(BF16)

    r = _dot(h_ref[...], w_ref[...])

    @pl.when(j < n32)
    def _():
        o32_ref[...] = r

    @pl.when(j >= n32)
    def _():
        o16_ref[...] = r.astype(BF16)


def _inproj(x2, mod_l, g, w_perm, seq):
    t, d = x2.shape
    bsz = mod_l.shape[0]
    steps_per_batch = seq // IN_TM
    w32 = len(IN_F32_SECTIONS) * D_MODEL
    w16 = len(IN_BF16_SECTIONS) * D_MODEL
    n32 = w32 // IN_TN
    n16 = w16 // IN_TN
    return pl.pallas_call(
        functools.partial(_inproj_kernel, n32),
        grid=(t // IN_TM, n32 + n16),
        in_specs=[
            pl.BlockSpec((IN_TM, d), lambda i, j: (i, 0)),
            pl.BlockSpec((1, 1, 3 * d), lambda i, j: (i // steps_per_batch, 0, 0)),
            pl.BlockSpec((1, d), lambda i, j: (0, 0)),
            pl.BlockSpec((d, IN_TN), lambda i, j: (0, j)),
        ],
        out_specs=[
            pl.BlockSpec((IN_TM, IN_TN), lambda i, j: (i, jnp.minimum(j, n32 - 1))),
            pl.BlockSpec((IN_TM, IN_TN), lambda i, j: (i, jnp.maximum(j - n32, 0))),
        ],
        out_shape=[jax.ShapeDtypeStruct((t, w32), F32), jax.ShapeDtypeStruct((t, w16), BF16)],
        scratch_shapes=[pltpu.VMEM((IN_TM, d), BF16)],
        compiler_params=pltpu.CompilerParams(
            dimension_semantics=("parallel", "arbitrary"), vmem_limit_bytes=VMEM_LIMIT),
        name="inproj",
    )(x2, mod_l.reshape(bsz, 1, 3 * d), g.reshape(1, d), w_perm)


def _permute_w_in(w):
    sec = [w[:, s * D_MODEL:(s + 1) * D_MODEL] for s in range(8)]
    sec[4] = sec[4] * (SB_DH ** -0.5 * LOG2E)
    cols = [sec[s] for s in IN_F32_SECTIONS + IN_BF16_SECTIONS]
    return jnp.concatenate(cols, axis=1).astype(BF16)


def _hgrn_tables():
    c = HG_C
    t = np.arange(c)[:, None]
    j = np.arange(c)[None, :]
    blocks = [(j <= t)]
    for h in HG_MXU_LEVELS:
        mid = (t // (2 * h)) * 2 * h + h - 1
        blocks.append(((j > mid) & (j <= t)) | ((j > t) & (j <= mid)))
    pmat = np.concatenate(blocks, axis=0).astype(np.float32)
    x = t ^ j
    lv = np.where(j > t, -1, np.where(j == t, 0, 1 + np.floor(np.log2(np.maximum(x, 1))))).astype(np.int32)
    return jnp.asarray(pmat, BF16), jnp.asarray(lv)


def _hgrn_kernel(layer, q_ref, f_ref, i_ref, g_ref, lbl_ref, gn_ref, pmat_ref, lv_ref,
                 o_ref, st_ref, d_ref, dec1_ref, kk_ref, hl_ref):
    c = HG_C
    heads = range(HG_HB)

    @pl.when(pl.program_id(2) == 0)
    def _():
        st_ref[...] = jnp.zeros_like(st_ref)

    if layer > 0:
        lg = lbl_ref[...]
        e = jnp.exp(lg - jnp.max(lg, axis=0, keepdims=True))
        p = e / jnp.sum(e, axis=0, keepdims=True)
        lb = p[1:2, :]
        for l in range(2, layer + 1):
            lb = lb + p[l:l + 1, :]
        log_lb = jnp.log(lb)
        log1m_lb = jnp.log(1.0 - lb)
        one_m_lb = 1.0 - lb

    gn = gn_ref[...]
    row_id = lax.broadcasted_iota(jnp.int32, (c, LANES), 0)
    half_sign = {h: jnp.where((row_id & h) != 0, 1.0, -1.0) for h in HG_VPU_LEVELS}

    w = HG_HB * LANES

    def gates(rows):
        fl = f_ref[rows, :]
        e = jnp.exp2(jnp.abs(fl) * -LOG2E)
        ope = 1.0 + e
        ls = jnp.minimum(fl, 0.0) - jnp.log(ope)
        r = 1.0 / ope
        sig = jnp.where(fl >= 0.0, 1.0, e) * r
        k_all = jnp.where(fl >= 0.0, e, 1.0) * r
        if layer == 0:
            return ls * LOG2E, sig, k_all
        bb = log1m_lb + ls
        lf = jnp.maximum(log_lb, bb) + jnp.log(1.0 + jnp.exp(-jnp.abs(log_lb - bb)))
        return lf * LOG2E, lb + one_m_lb * sig, one_m_lb * k_all

    chunks = range(HG_UNROLL)
    n_groups = HG_SB // (c * HG_UNROLL)
    odd_row_w = (lax.broadcasted_iota(jnp.int32, (c, w), 0) & 1) == 1

    def group_rows(gi):
        return [pl.ds(pl.multiple_of((gi * HG_UNROLL + t) * c, c), c) for t in chunks]

    def front(gi):
        rows = group_rows(gi)
        lfk = [gates(rows[t]) for t in chunks]
        for t in chunks:
            dec1_ref[t] = jnp.where(odd_row_w, lfk[t][1], 1.0).astype(BF16)
            kk_ref[t] = lfk[t][2].astype(BF16)
        splits = [_split_bf16(lfk[t][0]) for t in chunks]
        hl_ref[0] = jnp.concatenate([s[0] for s in splits], axis=1)
        hl_ref[1] = jnp.concatenate([s[1] for s in splits], axis=1)
        cum = pmat_ref[0:c, :]
        d_ref[0:c, :] = _dot(cum, hl_ref[0]) + _dot(cum, hl_ref[1])
        b3 = d_ref[0:c, :].reshape(c // HG_SUB, HG_SUB, HG_UNROLL * w)
        last = b3[:, HG_SUB - 1:HG_SUB, :]
        prev = jnp.concatenate([jnp.zeros_like(last[0:1]), last[:-1]], axis=0)
        return (jnp.max(prev - last) < HG_SAFE_LOG2).astype(jnp.int32)

    def back(gi, direct_sub):
        rows = group_rows(gi)
        lv = lv_ref[...]
        streams = [(t, hd) for t in chunks for hd in heads]
        sl_in = {s: slice(s[1] * LANES, (s[1] + 1) * LANES) for s in streams}
        sl_d = {s: slice(s[0] * w + s[1] * LANES, s[0] * w + (s[1] + 1) * LANES) for s in streams}
        qs = {s: _silu(q_ref[rows[s[0]], sl_in[s]]).astype(BF16) for s in streams}
        kk = {s: kk_ref[s[0], :, sl_in[s]] for s in streams}
        iv = {s: i_ref[rows[s[0]], sl_in[s]] for s in streams}
        bs = {s: d_ref[0:c, sl_d[s]] for s in streams}

        def add_level(idx, dec_of):
            decs = {s: dec_of(s) for s in streams}
            for s in streams:
                s_h = _dot_nt(qs[s] * decs[s], kk[s] * decs[s])
                sc[s] = jnp.where(lv == idx, s_h, sc[s])

        def block_level(h):
            def dec_of(s):
                b3 = bs[s].reshape(c // (2 * h), 2 * h, LANES)
                d = (b3 - b3[:, h - 1:h, :]).reshape(c, LANES) * half_sign[h]
                return jnp.exp2(d).astype(BF16)
            add_level(1 + int(math.log2(h)), dec_of)

        if direct_sub:
            sub_ops = {}
            for s in streams:
                b3 = bs[s].reshape(c // HG_SUB, HG_SUB, LANES)
                last = b3[:, HG_SUB - 1:HG_SUB, :]
                start = jnp.concatenate([jnp.zeros_like(last[0:1]), last[:-1]], axis=0)
                dq = (b3 - start).reshape(c, LANES)
                sub_ops[s] = (qs[s] * jnp.exp2(dq).astype(BF16), kk[s] * jnp.exp2(-dq).astype(BF16))
            in_sub = (lv >= 0) & (lv <= int(math.log2(HG_SUB)))
            sc = {s: jnp.where(in_sub, _dot_nt(*sub_ops[s]), 0.0) for s in streams}
            for h in HG_VPU_LEVELS:
                if h >= HG_SUB:
                    block_level(h)
        else:
            lower = pmat_ref[c:, :]
            d_ref[c:, :] = _dot(lower, hl_ref[0]) + _dot(lower, hl_ref[1])
            sc = {s: jnp.where(lv == 0, _dot_nt(qs[s], kk[s]), 0.0) for s in streams}
            add_level(1, lambda s: dec1_ref[s[0], :, sl_in[s]])
            for n, h in enumerate(HG_MXU_LEVELS):
                add_level(1 + int(math.log2(h)),
                          lambda s, n=n: jnp.exp2(d_ref[(n + 1) * c:(n + 2) * c, sl_d[s]]).astype(BF16))
            for h in HG_VPU_LEVELS:
                block_level(h)

        o_intra = {s: _dot(sc[s].astype(BF16), iv[s]) for s in streams}
        q_dec = {s: qs[s] * jnp.exp2(bs[s]).astype(BF16) for s in streams}
        b_last = {s: bs[s][c - 1:c, :] for s in streams}
        k_dec = {s: kk[s] * jnp.exp2(b_last[s] - bs[s]).astype(BF16) for s in streams}

        o_inter = {}
        for hd in heads:
            st = st_ref[hd]
            for t in chunks:
                s = (t, hd)
                o_inter[s] = _dot_nt(q_dec[s], st.astype(BF16))
                st = st * jnp.exp2(b_last[s]) + _dot_tn(iv[s], k_dec[s])
            st_ref[hd] = st

        next_safe = front(jnp.minimum(gi + 1, n_groups - 1))

        for s in streams:
            o = o_intra[s] + o_inter[s]
            ms = jnp.mean(o * o, axis=-1, keepdims=True)
            y = o * lax.rsqrt(ms + EPS) * gn * _silu(g_ref[rows[s[0]], sl_in[s]])
            o_ref[rows[s[0]], sl_in[s]] = y.astype(o_ref.dtype)
        return next_safe

    def chunk_group(gi, safe):
        return lax.cond(safe == 1, lambda: back(gi, True), lambda: back(gi, False))

    lax.fori_loop(0, n_groups, chunk_group, front(0))


def _hgrn(u32, u16, lb_logits, gn, layer, bsz, seq):
    t = u32.shape[0]
    w = HG_HB * LANES
    nsb = seq // HG_SB
    depth = lb_logits.shape[0]
    pmat, lv = _hgrn_tables()
    hpb = D_MODEL // w

    def col(section):
        return lambda b, h, s: (b * nsb + s, section * hpb + h)

    return pl.pallas_call(
        functools.partial(_hgrn_kernel, layer),
        grid=(bsz, HGRN_HEADS // HG_HB, nsb),
        in_specs=[
            pl.BlockSpec((HG_SB, w), col(0)),
            pl.BlockSpec((HG_SB, w), col(1)),
            pl.BlockSpec((HG_SB, w), col(0)),
            pl.BlockSpec((HG_SB, w), col(2)),
            pl.BlockSpec((depth, w), lambda b, h, s: (0, h)),
            pl.BlockSpec((1, LANES), lambda b, h, s: (0, 0)),
            pl.BlockSpec(pmat.shape, lambda b, h, s: (0, 0)),
            pl.BlockSpec(lv.shape, lambda b, h, s: (0, 0)),
        ],
        out_specs=pl.BlockSpec((HG_SB, w), lambda b, h, s: (b * nsb + s, h)),
        out_shape=jax.ShapeDtypeStruct((t, D_MODEL), BF16),
        scratch_shapes=[
            pltpu.VMEM((HG_HB, LANES, LANES), F32),
            pltpu.VMEM((pmat.shape[0], HG_UNROLL * w), F32),
            pltpu.VMEM((HG_UNROLL, HG_C, w), BF16),
            pltpu.VMEM((HG_UNROLL, HG_C, w), BF16),
            pltpu.VMEM((2, HG_C, HG_UNROLL * w), BF16),
        ],
        compiler_params=pltpu.CompilerParams(
            dimension_semantics=("parallel", "parallel", "arbitrary"),
            vmem_limit_bytes=VMEM_LIMIT),
        name="hgrn2",
    )(u32, u32, u16, u32, lb_logits, gn.reshape(1, LANES), pmat, lv)


def _attn_tables():
    j = np.arange(AT_TK)
    after = (j[:, None] >= j[None, :]).astype(np.float32)
    lane = np.arange(AT_HB * SB_DH)
    seg = (lane[:, None] // SB_DH == lane[None, :] // SB_DH).astype(np.float32) / SB_DH
    return jnp.asarray(after, BF16), jnp.asarray(seg, BF16)


def _attn_kernel(q_ref, k_ref, v_ref, g_ref, gn_ref, u_ref, seg_ref, o_ref,
                 vm_ref, acc_ref, car_ref, dead_ref):
    step = pl.program_id(2)
    heads = range(AT_HB)
    w = AT_HB * SB_DH
    lane = lax.broadcasted_iota(jnp.int32, (1, w), 1)
    head_lanes = [(lane >= hd * SB_DH) & (lane < (hd + 1) * SB_DH) for hd in heads]

    @pl.when(step == 0)
    def _():
        v = v_ref[...]
        for hd in heads:
            vm_ref[hd] = jnp.where(head_lanes[hd], v, jnp.zeros_like(v))

    umat = u_ref[...]
    row = lax.broadcasted_iota(jnp.int32, (AT_TQ, AT_TK), 0)
    colm = lax.broadcasted_iota(jnp.int32, (AT_TQ, AT_TK), 1)
    causal = colm < row

    def q_block(sub, carry):
        qi = step * AT_QB + sub
        qrows = pl.ds(pl.multiple_of(sub * AT_TQ, AT_TQ), AT_TQ)
        q = q_ref[qrows, :]
        qm = [jnp.where(head_lanes[hd], q, jnp.zeros_like(q)) for hd in heads]
        acc_ref[...] = jnp.zeros_like(acc_ref)
        car_ref[...] = jnp.zeros_like(car_ref)

        def tiles(js, masked):
            n = len(js)
            keys = [pl.ds(pl.multiple_of(j * AT_TK, AT_TK), AT_TK) for j in js]
            streams = [(t, hd) for t in range(n) for hd in heads]
            kb = [k_ref[ks, :] for ks in keys]
            zs = {s: _dot_nt(qm[s[1]], kb[s[0]]) for s in streams}
            parts = {}
            for s in streams:
                z = zs[s]
                sp = jnp.maximum(z, 0.0) + jnp.log(1.0 + jnp.exp2(-jnp.abs(z))) * LOG2E
                if masked[s[0]]:
                    sp = jnp.where(causal, sp, 0.0)
                parts[s] = _split_bf16(sp)
            rs = {s: _dot(parts[s][0], umat) + _dot(parts[s][1], umat) for s in streams}
            cars = {}
            low = None
            for hd in heads:
                car = car_ref[hd]
                for t in range(n):
                    cars[(t, hd)] = car
                    car = car + rs[(t, hd)][:, 0:1]
                car_ref[hd] = car
                low = car if low is None else jnp.minimum(low, car)
            dead_ref[0] = (jnp.min(low) >= AT_DEAD_LOG2).astype(jnp.int32)
            pv = None
            for s in streams:
                wgt = jnp.exp2(zs[s] - cars[s] - rs[s])
                if masked[s[0]]:
                    wgt = jnp.where(causal, wgt, 0.0)
                d = _dot(wgt.astype(BF16), vm_ref[s[1], keys[s[0]], :])
                pv = d if pv is None else pv + d
            acc_ref[...] += pv

        @pl.when(qi == 0)
        def _():
            tiles([qi], [True])

        @pl.when(qi > 0)
        def _():
            tiles([qi, qi - 1], [True, False])

        rem = jnp.maximum(qi - 1, 0)
        n_pairs = lax.shift_right_logical(rem, 1)

        def cond(state):
            it, dead = state
            return (it < n_pairs) & (dead == 0)

        def body(state):
            it, _ = state
            j = qi - 2 - 2 * it
            tiles([j, j - 1], [False, False])
            return it + 1, dead_ref[0]

        _, dead = lax.while_loop(cond, body, (jnp.int32(0), dead_ref[0]))

        @pl.when(((rem & 1) == 1) & (dead == 0))
        def _():
            tiles([0], [False])

        o = acc_ref[...]
        hi, lo = _split_bf16(o * o)
        seg = seg_ref[...]
        ms = _dot(hi, seg) + _dot(lo, seg)
        y = o * lax.rsqrt(ms + EPS) * gn_ref[...] * _silu(g_ref[qrows, :])
        o_ref[qrows, :] = y.astype(o_ref.dtype)
        return carry

    lax.fori_loop(0, AT_QB, q_block, 0)


def _attn(u32, u16, gn, bsz, seq):
    t = u32.shape[0]
    w = AT_HB * SB_DH
    rows = AT_QB * AT_TQ
    nsteps = seq // rows
    umat, seg = _attn_tables()
    hpb = D_MODEL // w
    gn2 = jnp.tile(gn.reshape(1, SB_DH), (1, AT_HB))

    return pl.pallas_call(
        _attn_kernel,
        grid=(bsz, SB_HEADS // AT_HB, nsteps),
        in_specs=[
            pl.BlockSpec((rows, w), lambda b, h, i: (b * nsteps + i, 1 * hpb + h)),
            pl.BlockSpec((seq, w), lambda b, h, i: (b, 2 * hpb + h)),
            pl.BlockSpec((seq, w), lambda b, h, i: (b, 3 * hpb + h)),
            pl.BlockSpec((rows, w), lambda b, h, i: (b * nsteps + i, 3 * hpb + h)),
            pl.BlockSpec((1, w), lambda b, h, i: (0, 0)),
            pl.BlockSpec(umat.shape, lambda b, h, i: (0, 0)),
            pl.BlockSpec(seg.shape, lambda b, h, i: (0, 0)),
        ],
        out_specs=pl.BlockSpec((rows, w), lambda b, h, i: (b * nsteps + i, h)),
        out_shape=jax.ShapeDtypeStruct((t, D_MODEL), BF16),
        scratch_shapes=[
            pltpu.VMEM((AT_HB, seq, w), BF16),
            pltpu.VMEM((AT_TQ, w), F32),
            pltpu.VMEM((AT_HB, AT_TQ, 1), F32),
            pltpu.SMEM((1,), jnp.int32),
        ],
        compiler_params=pltpu.CompilerParams(
            dimension_semantics=("parallel", "parallel", "arbitrary"),
            vmem_limit_bytes=VMEM_LIMIT),
        name="stickbreak",
    )(u16, u16, u16, u32, gn2, umat, seg)


def _outproj_kernel(final, oa_ref, ob_ref, w_ref, x_ref, mod_ref, fg_ref, o_ref):
    d = x_ref.shape[1]
    da = oa_ref.shape[1]
    y = _dot(oa_ref[...], w_ref[0:da, :]) + _dot(ob_ref[...], w_ref[da:, :])
    gate = mod_ref[0, :, 2 * d:3 * d]
    xn = x_ref[...] + gate * y
    if final:
        ms = jnp.mean(xn * xn, axis=-1, keepdims=True)
        xn = xn * lax.rsqrt(ms + EPS) * fg_ref[...]
    o_ref[...] = xn


def _outproj(o_a, o_b, w_bf16, x2, mod_l, final_g, final, seq):
    t, d = x2.shape
    bsz = mod_l.shape[0]
    steps_per_batch = seq // OUT_TM
    return pl.pallas_call(
        functools.partial(_outproj_kernel, final),
        grid=(t // OUT_TM,),
        in_specs=[
            pl.BlockSpec((OUT_TM, o_a.shape[1]), lambda i: (i, 0)),
            pl.BlockSpec((OUT_TM, o_b.shape[1]), lambda i: (i, 0)),
            pl.BlockSpec(w_bf16.shape, lambda i: (0, 0)),
            pl.BlockSpec((OUT_TM, d), lambda i: (i, 0)),
            pl.BlockSpec((1, 1, 3 * d), lambda i: (i // steps_per_batch, 0, 0)),
            pl.BlockSpec((1, d), lambda i: (0, 0)),
        ],
        out_specs=pl.BlockSpec((OUT_TM, d), lambda i: (i, 0)),
        out_shape=jax.ShapeDtypeStruct((t, d), F32),
        compiler_params=pltpu.CompilerParams(
            dimension_semantics=("parallel",), vmem_limit_bytes=VMEM_LIMIT),
        name="outproj",
    )(o_a, o_b, w_bf16, x2, mod_l.reshape(bsz, 1, 3 * d), final_g.reshape(1, d))


def kernel(x, c, w_ada, b_ada, norm_g, w_in, lb_logits, hgrn_norm_g, sb_norm_g, w_out, final_g):
    bsz, seq, d = x.shape
    depth = w_ada.shape[0]
    assert d == D_MODEL and w_in.shape[2] == D_IN
    assert seq % max(IN_TM, HG_SB, AT_QB * AT_TQ, OUT_TM) == 0 and AT_TQ == AT_TK
    mod = _adaln_mod(c, w_ada, b_ada)
    x2 = x.reshape(bsz * seq, d)
    for layer in range(depth):
        u32, u16 = _inproj(x2, mod[layer], norm_g[layer], _permute_w_in(w_in[layer]), seq)
        o_a = _hgrn(u32, u16, lb_logits, hgrn_norm_g[layer], layer, bsz, seq)
        o_b = _attn(u32, u16, sb_norm_g[layer], bsz, seq)
        x2 = _outproj(o_a, o_b, w_out[layer].astype(BF16), x2, mod[layer], final_g,
                      layer == depth - 1, seq)
    return x2.reshape(bsz, seq, d)
```
